```python
import jax
import jax.numpy as jnp
from jax import lax
import numpy as np

D_MODEL = 1024
BATCH = 4
SEQ = 4096
DEPTH = 2
DEC_BATCH = 32
DEC_SEQ = 8
PAST_LEN = 16384
PAGE_SIZE = 128

HEAD_DIM = 64
N_HEADS_TOTAL = D_MODEL // HEAD_DIM
N_HEADS_MLSTM = N_HEADS_TOTAL // 4
N_HEADS_ATTN = (N_HEADS_TOTAL - N_HEADS_MLSTM) // 2
N_CONV_GROUPS = N_HEADS_TOTAL - N_HEADS_MLSTM - N_HEADS_ATTN
D_ATTN = N_HEADS_ATTN * HEAD_DIM
D_MLSTM = N_HEADS_MLSTM * HEAD_DIM
D_CONV = N_CONV_GROUPS * HEAD_DIM
D_MIX = D_ATTN + D_MLSTM + D_CONV
DILATED_BRANCHES = ((128, 1), (512, 4), (2048, 16))
WINDOW = max(w for w, _ in DILATED_BRANCHES)
ATTN_BLOCK = 128
ROPE_DIM = HEAD_DIM // 4
ROPE_THETA = 500000.0
MLSTM_CHUNK = 128
CONV_WIDTH = 31
D_FF = 2816
N_EXPERTS = 8
TOP_K = 2
EPS = 1e-6
NEG = -1e30
SPLIT_SIZES = (D_ATTN, D_ATTN, D_ATTN, D_MLSTM, D_MLSTM, D_MLSTM, D_MLSTM,
               N_HEADS_MLSTM, N_HEADS_MLSTM, D_CONV, D_CONV)
D_IN = sum(SPLIT_SIZES)
N_DENSE_LAYERS = (DEPTH + 1) // 2
N_MOE_LAYERS = DEPTH // 2
F32 = jnp.float32

kernel_name = 'hybrid_dilated_mlstm_conformer_step'


def rms_f32(x, g):
    xf = x.astype(F32)
    return xf * lax.rsqrt(jnp.mean(xf * xf, axis=-1, keepdims=True) + EPS) * g.astype(F32)


def rmsnorm(x, g):
    return rms_f32(x, g).astype(x.dtype)


def rope(x, pos):
    half = ROPE_DIM // 2
    inv_freq = ROPE_THETA ** (-2.0 * jnp.arange(half, dtype=F32) / ROPE_DIM)
    ang = pos[:, None] * inv_freq[None, :]
    cos = jnp.cos(ang)[None, :, None, :]
    sin = jnp.sin(ang)[None, :, None, :]
    x1 = x[..., :half]
    x2 = x[..., half:ROPE_DIM]
    return jnp.concatenate([x1 * cos - x2 * sin, x2 * cos + x1 * sin, x[..., ROPE_DIM:]], axis=-1)


def mixer_inputs(h, w_in, q_norm, k_norm, gate_bias, pos):
    b, s = h.shape[0], h.shape[1]
    z = jnp.einsum('bsd,de->bse', h, w_in)
    points = np.cumsum(SPLIT_SIZES)[:-1].tolist()
    qa, ka, va, qm, km, vm, om, ig, fg, cv, cg = jnp.split(z, points, axis=-1)

    def heads(t, n):
        return t.astype(F32).reshape(b, s, n, HEAD_DIM)

    qa = rope(rms_f32(heads(qa, N_HEADS_ATTN), q_norm), pos)
    ka = rope(rms_f32(heads(ka, N_HEADS_ATTN), k_norm), pos)
    va = heads(va, N_HEADS_ATTN)
    gb = gate_bias.astype(F32)
    ig = ig.astype(F32) + gb[:N_HEADS_MLSTM]
    lf = jax.nn.log_sigmoid(fg.astype(F32) + gb[N_HEADS_MLSTM:])
    mlstm = (heads(qm, N_HEADS_MLSTM), heads(km, N_HEADS_MLSTM) * HEAD_DIM ** -0.5,
             heads(vm, N_HEADS_MLSTM), heads(om, N_HEADS_MLSTM), ig, lf)
    u = cv.astype(F32) * jax.nn.sigmoid(cg.astype(F32))
    return (qa, ka, va), mlstm, u


def dilated_branch_prompt(q, k, v, window, dil):
    b, s, h, e = q.shape
    w_sub = window // dil
    n_sub = s // dil
    nb = -(-n_sub // ATTN_BLOCK)
    lp = nb * ATTN_BLOCK

    def to_blocks(t):
        t = t.reshape(b, n_sub, dil, h, e).transpose(0, 2, 1, 3, 4)
        t = jnp.pad(t, ((0, 0), (0, 0), (0, lp - n_sub), (0, 0), (0, 0)))
        return t.reshape(b, dil, nb, ATTN_BLOCK, h, e)

    def with_prev(t):
        prev = jnp.pad(t[:, :, :-1], ((0, 0), (0, 0), (1, 0), (0, 0), (0, 0), (0, 0)))
        return jnp.concatenate([prev, t], axis=3)

    qb = to_blocks(q)
    kb = with_prev(to_blocks(k))
    vb = with_prev(to_blocks(v))
    sc = jnp.einsum('bgnqhe,bgnkhe->bgnhqk', qb, kb) * HEAD_DIM ** -0.5
    blk = jnp.arange(nb)[:, None, None]
    qi = blk * ATTN_BLOCK + jnp.arange(ATTN_BLOCK)[None, :, None]
    kj = (blk - 1) * ATTN_BLOCK + jnp.arange(2 * ATTN_BLOCK)[None, None, :]
    valid = (kj <= qi) & (kj >= qi - w_sub) & (kj >= 0)
    sc = jnp.where(valid[None, None, :, None], sc, NEG)
    mx = sc.max(-1)
    p = jnp.exp(sc - mx[..., None])
    den = p.sum(-1)
    acc = jnp.einsum('bgnhqk,bgnkhe->bgnqhe', p, vb)

    def from_blocks(t):
        t = t.reshape(b, dil, lp, *t.shape[4:])[:, :, :n_sub]
        t = jnp.swapaxes(t, 1, 2)
        return t.reshape(b, s, *t.shape[3:])

    return from_blocks(acc), from_blocks(jnp.swapaxes(mx, 3, 4)), from_blocks(jnp.swapaxes(den, 3, 4))


def dilated_branch_sample(q, kc, vc, window, dil):
    t = q.shape[1]
    p_len = kc.shape[1] - t
    n_keys = window // dil + 1
    idx = p_len + jnp.arange(t)[:, None] - dil * jnp.arange(n_keys)[None, :]
    valid = idx >= 0
    idx = jnp.maximum(idx, 0)
    kg = kc[:, idx]
    vg = vc[:, idx]
    sc = jnp.einsum('bthe,btkhe->bthk', q, kg) * HEAD_DIM ** -0.5
    sc = jnp.where(valid[None, :, None, :], sc, NEG)
    mx = sc.max(-1)
    p = jnp.exp(sc - mx[..., None])
    den = p.sum(-1)
    acc = jnp.einsum('bthk,btkhe->bthe', p, vg)
    return acc, mx, den


def combine_branches(branches):
    accs, mxs, dens = zip(*branches)
    mx = jnp.stack(mxs)
    wts = jnp.exp(mx - mx.max(0))
    num = jnp.einsum('gbsh,gbshe->bshe', wts, jnp.stack(accs))
    den = jnp.sum(wts * jnp.stack(dens), axis=0)
    return num / den[..., None]


def mlstm_chunk(carry, xs):
    c0, n0, m0 = carry
    q, k, v, ig, lf = xs
    L = q.shape[2]
    fcum = jnp.cumsum(lf, axis=-1)
    causal = jnp.tril(jnp.ones((L, L), dtype=bool))
    dmat = jnp.where(causal, fcum[..., :, None] - fcum[..., None, :] + ig[..., None, :], NEG)
    inter = fcum + m0[..., None]
    m = jnp.maximum(inter, dmat.max(-1))
    w = jnp.exp(dmat - m[..., None])
    sc = jnp.einsum('bhtd,bhsd->bhts', q, k) * w
    g = jnp.exp(inter - m)
    num = jnp.einsum('bhts,bhsd->bhtd', sc, v) + g[..., None] * jnp.einsum('bhtd,bhde->bhte', q, c0)
    den = sc.sum(-1) + g * jnp.einsum('bhtd,bhd->bht', q, n0)
    h = num / jnp.maximum(jnp.abs(den), jnp.exp(-m))[..., None]
    m_last = m[..., -1]
    decay = jnp.exp(fcum[..., -1] + m0 - m_last)
    wl = jnp.exp(fcum[..., -1:] - fcum + ig - m_last[..., None])
    c1 = decay[..., None, None] * c0 + jnp.einsum('bhs,bhsd,bhse->bhde', wl, k, v)
    n1 = decay[..., None] * n0 + jnp.einsum('bhs,bhsd->bhd', wl, k)
    return (c1, n1, m_last), h


def mlstm_prompt(q, k, v, ig, lf):
    b, s, h, e = q.shape
    nc = s // MLSTM_CHUNK

    def to_chunks(t):
        t = t.reshape(b, nc, MLSTM_CHUNK, h, *t.shape[3:])
        return jnp.moveaxis(t, (1, 3), (0, 2))

    carry0 = (jnp.zeros((b, h, e, e), F32), jnp.zeros((b, h, e), F32), jnp.zeros((b, h), F32))
    carry, hs = lax.scan(mlstm_chunk, carry0, tuple(to_chunks(t) for t in (q, k, v, ig, lf)))
    hs = jnp.moveaxis(hs, (0, 2), (1, 3)).reshape(b, s, h, e)
    return hs, carry


def mlstm_sample(q, k, v, ig, lf, c0, n0, m0):
    carry, h = mlstm_chunk((c0.astype(F32), n0.astype(F32), m0.astype(F32)),
                           (jnp.swapaxes(q, 1, 2), jnp.swapaxes(k, 1, 2), jnp.swapaxes(v, 1, 2),
                            jnp.swapaxes(ig, 1, 2), jnp.swapaxes(lf, 1, 2)))
    return jnp.swapaxes(h, 1, 2), carry


def depthwise_causal(u, w):
    return lax.conv_general_dilated(u, w.astype(u.dtype)[:, None, :], (1,), 'VALID',
                                    dimension_numbers=('NWC', 'WIO', 'NWC'),
                                    feature_group_count=u.shape[-1])


def mixer_output(att, hm, om, yc, out_norm, dw_b, cn_g, cn_b, w_out, dtype):
    b, s = att.shape[0], att.shape[1]
    a = att.reshape(b, s, D_ATTN)
    m = rms_f32(jax.nn.sigmoid(om) * hm, out_norm.reshape(N_HEADS_MLSTM, HEAD_DIM)).reshape(b, s, D_MLSTM)
    c = (yc + dw_b.astype(F32)).reshape(b, s, N_CONV_GROUPS, HEAD_DIM)
    mu = c.mean(-1, keepdims=True)
    var = jnp.mean(jnp.square(c - mu), axis=-1, keepdims=True)
    c = ((c - mu) * lax.rsqrt(var + EPS)).reshape(b, s, D_CONV) * cn_g.astype(F32) + cn_b.astype(F32)
    c = jax.nn.silu(c)
    cat = jnp.concatenate([a, m, c], axis=-1).astype(dtype)
    return jnp.einsum('bse,ed->bsd', cat, w_out)


def swiglu(h, w1, w3, w2):
    a = jnp.einsum('bsd,df->bsf', h, w1)
    g = jnp.einsum('bsd,df->bsf', h, w3)
    return jnp.einsum('bsf,fd->bsd', jax.nn.silu(a) * g, w2)


def moe(h, router, w1, w3, w2):
    logits = jnp.einsum('bsd,de->bse', h, router).astype(F32)
    top_v, top_i = lax.top_k(logits, TOP_K)
    gates = jax.nn.softmax(top_v, axis=-1)
    comb = jnp.sum(jax.nn.one_hot(top_i, N_EXPERTS, dtype=F32) * gates[..., None], axis=-2).astype(h.dtype)
    y = jnp.zeros_like(h)
    for e in range(N_EXPERTS):
        y = y + comb[..., e:e + 1] * swiglu(h, w1[e], w3[e], w2[e])
    return y


def channel_mixer(layer, h, ffn_w1, ffn_w3, ffn_w2, moe_router, moe_w1, moe_w3, moe_w2):
    i = layer // 2
    if layer % 2 == 0:
        return swiglu(h, ffn_w1[i], ffn_w3[i], ffn_w2[i])
    return moe(h, moe_router[i], moe_w1[i], moe_w3[i], moe_w2[i])


def setup_inputs(seed: int = 0) -> dict:
    key = jax.random.key(seed)
    keys = iter(list(jax.random.split(key, 40)))

    def nrm(shape, scale=1.0):
        return jax.random.normal(next(keys), shape, F32) * scale

    def gain(shape):
        return 1.0 + nrm(shape, 0.02)

    p_buf = min(WINDOW, PAST_LEN)
    gate_bias = jnp.concatenate(
        [nrm((DEPTH, N_HEADS_MLSTM), 0.1),
         jnp.linspace(3.0, 6.0, N_HEADS_MLSTM, dtype=F32)[None, :] + nrm((DEPTH, N_HEADS_MLSTM), 0.1)], axis=-1)
    return {
        'x_prompt': nrm((BATCH, SEQ, D_MODEL)),
        'x_sample': nrm((DEC_BATCH, DEC_SEQ, D_MODEL)),
        'cache_attn_k': nrm((DEPTH, DEC_BATCH, p_buf, N_HEADS_ATTN, HEAD_DIM)),
        'cache_attn_v': nrm((DEPTH, DEC_BATCH, p_buf, N_HEADS_ATTN, HEAD_DIM)),
        'state_mlstm_C': nrm((DEPTH, DEC_BATCH, N_HEADS_MLSTM, HEAD_DIM, HEAD_DIM), 0.1),
        'state_mlstm_n': nrm((DEPTH, DEC_BATCH, N_HEADS_MLSTM, HEAD_DIM), 0.1),
        'state_mlstm_m': nrm((DEPTH, DEC_BATCH, N_HEADS_MLSTM), 0.5),
        'state_conv': nrm((DEPTH, DEC_BATCH, CONV_WIDTH - 1, D_CONV), 0.5),
        'norm_mix': gain((DEPTH, D_MODEL)),
        'w_in': nrm((DEPTH, D_MODEL, D_IN), D_MODEL ** -0.5),
        'q_norm': gain((DEPTH, HEAD_DIM)),
        'k_norm': gain((DEPTH, HEAD_DIM)),
        'mlstm_gate_bias': gate_bias,
        'mlstm_out_norm': gain((DEPTH, D_MLSTM)),
        'conv_dw_w': nrm((DEPTH, CONV_WIDTH, D_CONV), CONV_WIDTH ** -0.5),
        'conv_dw_b': nrm((DEPTH, D_CONV), 0.02),
        'conv_norm_g': gain((DEPTH, D_CONV)),
        'conv_norm_b': nrm((DEPTH, D_CONV), 0.02),
        'w_out': nrm((DEPTH, D_MIX, D_MODEL), D_MIX ** -0.5),
        'norm_ffn': gain((DEPTH, D_MODEL)),
        'ffn_w1': nrm((N_DENSE_LAYERS, D_MODEL, D_FF), D_MODEL ** -0.5),
        'ffn_w3': nrm((N_DENSE_LAYERS, D_MODEL, D_FF), D_MODEL ** -0.5),
        'ffn_w2': nrm((N_DENSE_LAYERS, D_FF, D_MODEL), D_FF ** -0.5),
        'moe_router': nrm((N_MOE_LAYERS, D_MODEL, N_EXPERTS), D_MODEL ** -0.5),
        'moe_w1': nrm((N_MOE_LAYERS, N_EXPERTS, D_MODEL, D_FF), D_MODEL ** -0.5),
        'moe_w3': nrm((N_MOE_LAYERS, N_EXPERTS, D_MODEL, D_FF), D_MODEL ** -0.5),
        'moe_w2': nrm((N_MOE_LAYERS, N_EXPERTS, D_FF, D_MODEL), D_FF ** -0.5),
    }


def reference(x_prompt, x_sample, cache_attn_k, cache_attn_v, state_mlstm_C, state_mlstm_n, state_mlstm_m,
              state_conv, norm_mix, w_in, q_norm, k_norm, mlstm_gate_bias, mlstm_out_norm, conv_dw_w, conv_dw_b,
              conv_norm_g, conv_norm_b, w_out, norm_ffn, ffn_w1, ffn_w3, ffn_w2, moe_router, moe_w1, moe_w3,
              moe_w2):
    dtype = x_prompt.dtype
    seq = x_prompt.shape[1]
    dec = x_sample.shape[1]
    pos_p = jnp.arange(seq, dtype=F32)
    pos_s = PAST_LEN + jnp.arange(dec, dtype=F32)
    keep = min(WINDOW, seq)
    xp, xs = x_prompt, x_sample
    kp, vp, ksm, vsm, cpm, npm, mpm, csm, nsm, msm, cvp, cvs = [[] for _ in range(12)]

    for l in range(DEPTH):
        h = rmsnorm(xp, norm_mix[l])
        (qa, ka, va), (qm, km, vm, om, ig, lf), u = mixer_inputs(h, w_in[l], q_norm[l], k_norm[l],
                                                                 mlstm_gate_bias[l], pos_p)
        att = combine_branches([dilated_branch_prompt(qa, ka, va, w, d) for (w, d) in DILATED_BRANCHES])
        hm, (c1, n1, m1) = mlstm_prompt(qm, km, vm, ig, lf)
        yc = depthwise_causal(jnp.pad(u, ((0, 0), (CONV_WIDTH - 1, 0), (0, 0))), conv_dw_w[l])
        xp = xp + mixer_output(att, hm, om, yc, mlstm_out_norm[l], conv_dw_b[l], conv_norm_g[l],
                               conv_norm_b[l], w_out[l], dtype)
        xp = xp + channel_mixer(l, rmsnorm(xp, norm_ffn[l]), ffn_w1, ffn_w3, ffn_w2,
                                moe_router, moe_w1, moe_w3, moe_w2)
        kp.append(ka[:, seq - keep:])
        vp.append(va[:, seq - keep:])
        cpm.append(c1)
        npm.append(n1)
        mpm.append(m1)
        cvp.append(u[:, seq - (CONV_WIDTH - 1):])

        h = rmsnorm(xs, norm_mix[l])
        (qa, ka, va), (qm, km, vm, om, ig, lf), u = mixer_inputs(h, w_in[l], q_norm[l], k_norm[l],
                                                                 mlstm_gate_bias[l], pos_s)
        kc = jnp.concatenate([cache_attn_k[l].astype(F32), ka], axis=1)
        vc = jnp.concatenate([cache_attn_v[l].astype(F32), va], axis=1)
        att = combine_branches([dilated_branch_sample(qa, kc, vc, w, d) for (w, d) in DILATED_BRANCHES])
        hm, (c1, n1, m1) = mlstm_sample(qm, km, vm, ig, lf, state_mlstm_C[l], state_mlstm_n[l], state_mlstm_m[l])
        ucat = jnp.concatenate([state_conv[l].astype(F32), u], axis=1)
        yc = depthwise_causal(ucat, conv_dw_w[l])
        xs = xs + mixer_output(att, hm, om, yc, mlstm_out_norm[l], conv_dw_b[l], conv_norm_g[l],
                               conv_norm_b[l], w_out[l], dtype)
        xs = xs + channel_mixer(l, rmsnorm(xs, norm_ffn[l]), ffn_w1, ffn_w3, ffn_w2,
                                moe_router, moe_w1, moe_w3, moe_w2)
        ksm.append(ka)
        vsm.append(va)
        csm.append(c1)
        nsm.append(n1)
        msm.append(m1)
        cvs.append(ucat[:, ucat.shape[1] - (CONV_WIDTH - 1):])

    def stack(lst):
        return jnp.stack(lst, axis=0).astype(dtype)

    y_prompt = xp
    y_sample = xs
    return (y_prompt, y_sample, stack(kp), stack(vp), stack(ksm), stack(vsm),
            stack(cpm), stack(npm), stack(mpm), stack(csm), stack(nsm), stack(msm),
            stack(cvp), stack(cvs))
```

```python
import functools

import numpy as np
import jax
import jax.numpy as jnp
from jax import lax
from jax.experimental import pallas as pl
from jax.experimental.pallas import tpu as pltpu

F32 = jnp.float32
BF16 = jnp.bfloat16

D_MODEL = 1024
HEAD_DIM = 64
N_HEADS_ATTN = 6
N_HEADS_MLSTM = 4
D_ATTN = N_HEADS_ATTN * HEAD_DIM
D_MLSTM = N_HEADS_MLSTM * HEAD_DIM
D_CONV = 384
DILATED_BRANCHES = ((128, 1), (512, 4), (2048, 16))
ATTN_BLOCK = 128
ROPE_DIM = HEAD_DIM // 4
ROPE_THETA = 500000.0
MLSTM_CHUNK = 128
CONV_WIDTH = 31
D_FF = 2816
N_EXPERTS = 8
EPS = 1e-6
NEG = -1e30
PAST_LEN = 16384

LANES = 128
HIST_PAD = 32
GATE_PAD = LANES
W_IN_COLS = 3 * D_ATTN + 4 * D_MLSTM + GATE_PAD + 2 * D_CONV
VMEM_LIMIT = 56 * 1024 * 1024


def _tile(n, pref, mult=8):
    for t in range(min(pref, n), 0, -1):
        if n % t == 0 and t % mult == 0:
            return t
    return n


def _params(sem):
    return pltpu.CompilerParams(dimension_semantics=sem, vmem_limit_bytes=VMEM_LIMIT)


def _split_dot(a, b_bf16):
    hi = a.astype(BF16)
    lo = (a - hi.astype(F32)).astype(BF16)
    return (jnp.dot(hi, b_bf16, preferred_element_type=F32)
            + jnp.dot(lo, b_bf16, preferred_element_type=F32))


def _group_mean(a, bd_ref):
    return _split_dot(a, bd_ref[...])


def _inproj_kernel(x_ref, g_ref, w_ref, qn_ref, kn_ref, gb_ref, bd_ref, rc_ref, rs1_ref, rs2_ref,
                   q_ref, k_ref, v_ref, zm_ref, gt_ref, u_ref):
    x = x_ref[...]
    h = (x * lax.rsqrt(jnp.mean(x * x, axis=-1, keepdims=True) + EPS) * g_ref[...]).astype(BF16)

    def proj(lo, hi):
        return jnp.dot(h, w_ref[:, lo:hi], preferred_element_type=F32)

    rc, rs1, rs2 = rc_ref[...], rs1_ref[...], rs2_ref[...]

    def norm_rope(z, gain_ref, scale):
        zn = z * lax.rsqrt(_group_mean(z * z, bd_ref) + EPS) * gain_ref[...]
        outs = []
        for c in range(D_ATTN // LANES):
            zc = zn[:, c * LANES:(c + 1) * LANES]
            y = (zc * rc + pltpu.roll(zc, ROPE_DIM // 2, 1) * rs1
                 + pltpu.roll(zc, LANES - ROPE_DIM // 2, 1) * rs2)
            outs.append(y * scale if scale != 1.0 else y)
        return jnp.concatenate(outs, axis=1)

    q_ref[...] = norm_rope(proj(0, D_ATTN), qn_ref, HEAD_DIM ** -0.5)
    k_ref[...] = norm_rope(proj(D_ATTN, 2 * D_ATTN), kn_ref, 1.0)
    v_ref[...] = proj(2 * D_ATTN, 3 * D_ATTN)

    o = 3 * D_ATTN
    zm = proj(o, o + 4 * D_MLSTM)
    lane = lax.broadcasted_iota(jnp.int32, (1, 4 * D_MLSTM), 1)
    zm_ref[...] = zm * jnp.where((lane >= D_MLSTM) & (lane < 2 * D_MLSTM), HEAD_DIM ** -0.5, 1.0)

    o += 4 * D_MLSTM
    gt = proj(o, o + GATE_PAD) + gb_ref[...]
    glane = lax.broadcasted_iota(jnp.int32, (1, GATE_PAD), 1)
    log_sig = jnp.minimum(gt, 0.0) - jnp.log1p(jnp.exp(-jnp.abs(gt)))
    gt_ref[...] = jnp.where(glane < N_HEADS_MLSTM, gt, log_sig)

    o += GATE_PAD
    cv = proj(o, o + D_CONV)
    cg = proj(o + D_CONV, o + 2 * D_CONV)
    u_ref[...] = cv * jax.nn.sigmoid(cg)


def _inproj(x, g, w_p, qn_row, kn_row, gb_row, bd, rc, rs1, rs2):
    T = x.shape[0]
    tm = _tile(T, 640, LANES)
    row = lambda i: (i, 0)
    const = lambda i: (0, 0)
    out_shapes = [jax.ShapeDtypeStruct((T, n), F32)
                  for n in (D_ATTN, D_ATTN, D_ATTN, 4 * D_MLSTM, GATE_PAD, D_CONV)]
    return pl.pallas_call(
        _inproj_kernel,
        grid=(T // tm,),
        in_specs=[pl.BlockSpec((tm, D_MODEL), row),
                  pl.BlockSpec((1, D_MODEL), const),
                  pl.BlockSpec((D_MODEL, W_IN_COLS), const),
                  pl.BlockSpec((1, D_ATTN), const),
                  pl.BlockSpec((1, D_ATTN), const),
                  pl.BlockSpec((1, GATE_PAD), const),
                  pl.BlockSpec((D_ATTN, D_ATTN), const),
                  pl.BlockSpec((tm, LANES), row),
                  pl.BlockSpec((tm, LANES), row),
                  pl.BlockSpec((tm, LANES), row)],
        out_specs=[pl.BlockSpec((tm, s.shape[1]), row) for s in out_shapes],
        out_shape=out_shapes,
        compiler_params=_params(("parallel",)),
        name="inproj",
    )(x, g, w_p, qn_row, kn_row, gb_row, bd, rc, rs1, rs2)


def _attn_prompt_kernel(q_ref, k_ref, v_ref, o_ref, qs, ks, vs, racc, rm, rden, nacc, nm, nden, *, seq):
    blk = ATTN_BLOCK
    n_blocks = seq // blk
    lane = lax.broadcasted_iota(jnp.int32, (1, LANES), 1)
    head_a = lane < HEAD_DIM
    row = lax.broadcasted_iota(jnp.int32, (blk, 2 * blk), 0)
    col = lax.broadcasted_iota(jnp.int32, (blk, 2 * blk), 1)
    band = (col >= row) & (col <= row + blk)
    bias_band = jnp.where(band, 0.0, NEG)
    bias_first = jnp.where(band & (col >= blk), 0.0, NEG)

    ks[0:blk, :] = jnp.zeros((blk, LANES), BF16)
    vs[0:blk, :] = jnp.zeros((blk, LANES), BF16)

    cp = 256
    for (_, dil) in DILATED_BRANCHES:
        nsub = seq // dil
        nb = nsub // blk
        for r in range(dil):
            for c0 in range(0, nsub, cp):
                n = min(cp, nsub - c0)
                if dil == 1:
                    src = pl.ds(c0, n)
                else:
                    src = pl.ds(r + c0 * dil, n, stride=dil)
                dst = r * nsub + c0
                qs[dst:dst + n, :] = q_ref[src, :].astype(BF16)
                ks[blk + dst:blk + dst + n, :] = k_ref[src, :].astype(BF16)
                vs[blk + dst:blk + dst + n, :] = v_ref[src, :].astype(BF16)

        if dil == 1:
            dst_acc, dst_m, dst_den = nacc, nm, nden
        else:
            dst_acc, dst_m, dst_den = racc, rm, rden

        def body(j, carry, nb=nb, dst_acc=dst_acc, dst_m=dst_m, dst_den=dst_den):
            off = pl.multiple_of(j * blk, blk)
            qb = qs[pl.ds(off, blk), :]
            zero = jnp.zeros_like(qb)
            q2 = jnp.concatenate([jnp.where(head_a, qb, zero), jnp.where(head_a, zero, qb)], axis=0)
            kk = ks[pl.ds(off, 2 * blk), :]
            vv = vs[pl.ds(off, 2 * blk), :]
            s = lax.dot_general(q2, kk, (((1,), (1,)), ((), ())), preferred_element_type=F32)
            bias = jnp.where(j % nb == 0, bias_first, bias_band)
            s = s + jnp.concatenate([bias, bias], axis=0)
            mx = jnp.max(s, axis=1, keepdims=True)
            p = jnp.exp(s - mx)
            den = jnp.sum(p, axis=1, keepdims=True)
            o = jnp.dot(p.astype(BF16), vv, preferred_element_type=F32)
            dst_acc[pl.ds(off, blk), :] = jnp.where(head_a, o[:blk], o[blk:])
            dst_m[pl.ds(off, blk), :] = jnp.where(head_a, mx[:blk], mx[blk:])
            dst_den[pl.ds(off, blk), :] = jnp.where(head_a, den[:blk], den[blk:])
            return carry

        lax.fori_loop(0, n_blocks, body, 0)

        if dil > 1:
            for r in range(dil):
                for c0 in range(0, nsub, cp):
                    n = min(cp, nsub - c0)
                    nat = pl.ds(r + c0 * dil, n, stride=dil)
                    res = pl.ds(r * nsub + c0, n)
                    m_old, m_new = nm[nat, :], rm[res, :]
                    m_all = jnp.maximum(m_old, m_new)
                    e_old = jnp.exp(m_old - m_all)
                    e_new = jnp.exp(m_new - m_all)
                    nacc[nat, :] = e_old * nacc[nat, :] + e_new * racc[res, :]
                    nden[nat, :] = e_old * nden[nat, :] + e_new * rden[res, :]
                    nm[nat, :] = m_all

    o_ref[...] = nacc[...] / nden[...]


def _attn_prompt(q, k, v, batch, seq):
    n_pairs = D_ATTN // LANES
    spec = pl.BlockSpec((seq, LANES), lambda b, hp: (b, hp))
    return pl.pallas_call(
        functools.partial(_attn_prompt_kernel, seq=seq),
        grid=(batch, n_pairs),
        in_specs=[spec, spec, spec],
        out_specs=spec,
        out_shape=jax.ShapeDtypeStruct((batch * seq, D_ATTN), F32),
        scratch_shapes=[pltpu.VMEM((seq, LANES), BF16),
                        pltpu.VMEM((seq + ATTN_BLOCK, LANES), BF16),
                        pltpu.VMEM((seq + ATTN_BLOCK, LANES), BF16)]
                       + [pltpu.VMEM((seq, LANES), F32) for _ in range(6)],
        compiler_params=_params(("parallel", "parallel")),
        name="attn_prompt",
    )(q, k, v)


def _attn_sample_kernel(q_ref, kn_ref, vn_ref, kc_ref, vc_ref, cw_ref, o_ref, kall, vall, *, p_len, dec):
    pad_rows = kall.shape[0] - p_len
    kall[0:p_len, :] = kc_ref[...].astype(BF16)
    vall[0:p_len, :] = vc_ref[...].astype(BF16)
    tail = jnp.zeros((pad_rows - dec, D_ATTN), F32)
    kall[p_len:, :] = jnp.concatenate([kn_ref[...], tail], axis=0).astype(BF16)
    vall[p_len:, :] = jnp.concatenate([vn_ref[...], tail], axis=0).astype(BF16)

    lane = lax.broadcasted_iota(jnp.int32, (1, LANES), 1)
    head_a = lane < HEAD_DIM
    cw = cw_ref[...]
    cw2 = jnp.concatenate([cw, cw], axis=0)
    outs = []
    for hp in range(D_ATTN // LANES):
        sl = slice(hp * LANES, (hp + 1) * LANES)
        qb = q_ref[:, sl].astype(BF16)
        zero = jnp.zeros_like(qb)
        q2 = jnp.concatenate([jnp.where(head_a, qb, zero), jnp.where(head_a, zero, qb)], axis=0)
        s = lax.dot_general(q2, kall[:, sl], (((1,), (1,)), ((), ())), preferred_element_type=F32)
        s = jnp.where(cw2 > 0.0, s, NEG)
        mx = jnp.max(s, axis=1, keepdims=True)
        p = jnp.exp(s - mx) * cw2
        den = jnp.sum(p, axis=1, keepdims=True)
        o = jnp.dot(p.astype(BF16), vall[:, sl], preferred_element_type=F32) / den
        outs.append(jnp.where(head_a, o[:dec], o[dec:]))
    o_ref[...] = jnp.concatenate(outs, axis=1)


def _sample_multiplicity(p_len, dec, n_rows):
    t = np.arange(dec)[:, None]
    i = np.arange(n_rows)[None, :]
    dist = p_len + t - i
    cw = np.zeros((dec, n_rows), np.float32)
    for (w, d) in DILATED_BRANCHES:
        cw += ((dist >= 0) & (dist <= w) & (dist % d == 0) & (i < p_len + dec))
    return jnp.asarray(cw)


def _attn_sample(q, k, v, cache_k, cache_v, layer):
    nb, dec, _ = q.shape
    p_len = cache_k.shape[2]
    n_rows = p_len + LANES
    cw = _sample_multiplicity(p_len, dec, n_rows)
    new = pl.BlockSpec((None, dec, D_ATTN), lambda b: (b, 0, 0))
    cache = pl.BlockSpec((None, None, p_len, D_ATTN), lambda b: (layer, b, 0, 0))
    return pl.pallas_call(
        functools.partial(_attn_sample_kernel, p_len=p_len, dec=dec),
        grid=(nb,),
        in_specs=[new, new, new, cache, cache, pl.BlockSpec((dec, n_rows), lambda b: (0, 0))],
        out_specs=new,
        out_shape=jax.ShapeDtypeStruct((nb, dec, D_ATTN), F32),
        scratch_shapes=[pltpu.VMEM((n_rows, D_ATTN), BF16), pltpu.VMEM((n_rows, D_ATTN), BF16)],
        compiler_params=_params(("parallel",)),
        name="attn_sample",
    )(q, k, v, cache_k, cache_v, cw)


def _mlstm_kernel(zm_ref, gt_ref, c0_ref, nm0_ref, on_ref, h_ref, c1_ref, nm1_ref, c_s, nm_s, *, rows):
    ch = MLSTM_CHUNK
    step = pl.program_id(1)

    @pl.when(step == 0)
    def _():
        c_s[...] = c0_ref[...]
        nm_s[...] = nm0_ref[...]

    def padded(a):
        if rows == ch:
            return a
        return jnp.concatenate([a, jnp.zeros((ch - rows, a.shape[1]), a.dtype)], axis=0)

    lane = lax.broadcasted_iota(jnp.int32, (1, LANES), 1)
    head_a = lane < HEAD_DIM
    ri = lax.broadcasted_iota(jnp.int32, (ch, ch), 0)
    ci = lax.broadcasted_iota(jnp.int32, (ch, ch), 1)
    causal = ci <= ri
    eye = ci == ri
    rowv = lax.broadcasted_iota(jnp.int32, (ch, 1), 0)
    valid = rowv < rows

    gt = padded(gt_ref[...])
    glane = lax.broadcasted_iota(jnp.int32, (1, LANES), 1)
    is_ig = glane < N_HEADS_MLSTM
    gt = jnp.where(valid, gt, jnp.where(is_ig, NEG, 0.0))
    lf_only = jnp.where(is_ig, 0.0, gt)
    fcum = jnp.dot(causal.astype(F32), lf_only, preferred_element_type=F32,
                   precision=lax.Precision.HIGHEST)

    nm = nm_s[...]
    m_row_new = nm[2:3, :]
    brow128 = lax.broadcasted_iota(jnp.int32, (LANES, 1), 0) < HEAD_DIM
    bd_mask = brow128 == head_a

    for pair in range(N_HEADS_MLSTM // 2):
        ls = slice(pair * LANES, (pair + 1) * LANES)
        qp = padded(zm_ref[:, ls])
        kp = padded(zm_ref[:, 2 * LANES + pair * LANES:2 * LANES + (pair + 1) * LANES])
        vp = padded(zm_ref[:, 4 * LANES + pair * LANES:4 * LANES + (pair + 1) * LANES])
        op = padded(zm_ref[:, 6 * LANES + pair * LANES:6 * LANES + (pair + 1) * LANES])
        kb, vb = kp.astype(BF16), vp.astype(BF16)
        c_pair = c_s[pair]
        cb = c_pair.astype(BF16)
        n_row = nm[pair:pair + 1, :]

        h_heads, wl_heads, decays = [], [], []
        for sub in range(2):
            hx = 2 * pair + sub
            mask = head_a if sub == 0 else jnp.logical_not(head_a)
            a = fcum[:, N_HEADS_MLSTM + hx:N_HEADS_MLSTM + hx + 1]
            igc = gt[:, hx:hx + 1]
            m0 = nm[2:3, hx:hx + 1]
            brow = jnp.sum(jnp.where(eye, igc - a, 0.0), axis=0, keepdims=True)
            dmat = jnp.where(causal, a + brow, NEG)
            inter = a + m0
            m = jnp.maximum(inter, jnp.max(dmat, axis=1, keepdims=True))
            w = jnp.exp(dmat - m)
            g = jnp.exp(inter - m)
            qx = jnp.where(mask, qp, 0.0)
            qxb = qx.astype(BF16)
            sc = lax.dot_general(qxb, kb, (((1,), (1,)), ((), ())), preferred_element_type=F32) * w
            num = (jnp.dot(sc.astype(BF16), vb, preferred_element_type=F32)
                   + g * jnp.dot(qxb, cb, preferred_element_type=F32))
            den = (jnp.sum(sc, axis=1, keepdims=True)
                   + g * jnp.sum(qx * n_row, axis=1, keepdims=True))
            h_heads.append(num / jnp.maximum(jnp.abs(den), jnp.exp(-m)))
            m_last = m[rows - 1:rows, :]
            f_last = a[rows - 1:rows, :]
            decays.append(jnp.exp(f_last + m0 - m_last))
            wl_heads.append(jnp.exp(f_last - a + igc - m_last))
            m_row_new = jnp.where(glane == hx, m_last, m_row_new)

        kw = kp * jnp.where(head_a, wl_heads[0], wl_heads[1])
        upd = lax.dot_general(kw.astype(BF16), vb, (((0,), (0,)), ((), ())), preferred_element_type=F32)
        c_s[pair] = jnp.where(bd_mask, jnp.where(brow128, decays[0], decays[1]) * c_pair + upd, 0.0)
        n_new = jnp.where(head_a, decays[0], decays[1]) * n_row + jnp.sum(kw, axis=0, keepdims=True)
        nm_s[pair:pair + 1, :] = n_new

        y = jax.nn.sigmoid(op) * jnp.where(head_a, h_heads[0], h_heads[1])
        y2 = y * y
        ss_a = jnp.sum(jnp.where(head_a, y2, 0.0), axis=1, keepdims=True)
        ss_b = jnp.sum(jnp.where(head_a, 0.0, y2), axis=1, keepdims=True)
        ms = jnp.where(head_a, ss_a, ss_b) * (1.0 / HEAD_DIM)
        out = y * lax.rsqrt(ms + EPS) * on_ref[:, ls]
        h_ref[:, ls] = out[:rows]

    nm_s[2:3, :] = m_row_new

    @pl.when(step == pl.num_programs(1) - 1)
    def _():
        c1_ref[...] = c_s[...]
        nm1_ref[...] = nm_s[...]


def _mlstm(zm, gt, c0, nm0, on_row, nb, seq):
    rows = min(seq, MLSTM_CHUNK)
    n_chunks = seq // rows
    tok = lambda b, c: (b * n_chunks + c, 0)
    state = lambda b, c: (b, 0, 0, 0)
    return pl.pallas_call(
        functools.partial(_mlstm_kernel, rows=rows),
        grid=(nb, n_chunks),
        in_specs=[pl.BlockSpec((rows, 4 * D_MLSTM), tok),
                  pl.BlockSpec((rows, GATE_PAD), tok),
                  pl.BlockSpec((None, 2, LANES, LANES), state),
                  pl.BlockSpec((None, 8, LANES), lambda b, c: (b, 0, 0)),
                  pl.BlockSpec((1, D_MLSTM), lambda b, c: (0, 0))],
        out_specs=[pl.BlockSpec((rows, D_MLSTM), tok),
                   pl.BlockSpec((None, 2, LANES, LANES), state),
                   pl.BlockSpec((None, 8, LANES), lambda b, c: (b, 0, 0))],
        out_shape=[jax.ShapeDtypeStruct((nb * seq, D_MLSTM), F32),
                   jax.ShapeDtypeStruct((nb, 2, LANES, LANES), F32),
                   jax.ShapeDtypeStruct((nb, 8, LANES), F32)],
        scratch_shapes=[pltpu.VMEM((2, LANES, LANES), F32), pltpu.VMEM((8, LANES), F32)],
        compiler_params=_params(("parallel", "arbitrary")),
        name="mlstm",
    )(zm, gt, c0, nm0, on_row)


def _conv_kernel(u_ref, hist_ref, w_ref, b_ref, g_ref, nb_ref, bd_ref, c_ref, st_ref, uc, *, seq, rt):
    lead = HIST_PAD - (CONV_WIDTH - 1)
    uc[0:HIST_PAD, :] = hist_ref[...]
    uc[HIST_PAD:, :] = u_ref[...]
    w = w_ref[...]

    def chunk(c, carry):
        base = pl.multiple_of(c * rt, rt)
        win = uc[pl.ds(base, rt + HIST_PAD), :]
        acc = jnp.zeros((rt, D_CONV), F32)
        for j in range(CONV_WIDTH):
            acc = acc + win[lead + j:lead + j + rt, :] * w[j:j + 1, :]
        y = acc + b_ref[...]
        mu = _group_mean(y, bd_ref)
        yc = y - mu
        var = _group_mean(yc * yc, bd_ref)
        z = yc * lax.rsqrt(var + EPS) * g_ref[...] + nb_ref[...]
        c_ref[pl.ds(base, rt), :] = z * jax.nn.sigmoid(z)
        return carry

    lax.fori_loop(0, seq // rt, chunk, 0)
    st_ref[...] = uc[seq:seq + HIST_PAD, :]


def _conv(u, hist, w, b, g, nbias, bd, nb, seq):
    rt = min(seq, 256)
    tok = lambda i: (i, 0)
    const = lambda i: (0, 0)
    hspec = pl.BlockSpec((None, HIST_PAD, D_CONV), lambda i: (i, 0, 0))
    return pl.pallas_call(
        functools.partial(_conv_kernel, seq=seq, rt=rt),
        grid=(nb,),
        in_specs=[pl.BlockSpec((seq, D_CONV), tok), hspec,
                  pl.BlockSpec((HIST_PAD, D_CONV), const),
                  pl.BlockSpec((1, D_CONV), const), pl.BlockSpec((1, D_CONV), const),
                  pl.BlockSpec((1, D_CONV), const), pl.BlockSpec((D_CONV, D_CONV), const)],
        out_specs=[pl.BlockSpec((seq, D_CONV), tok), hspec],
        out_shape=[jax.ShapeDtypeStruct((nb * seq, D_CONV), F32),
                   jax.ShapeDtypeStruct((nb, HIST_PAD, D_CONV), F32)],
        scratch_shapes=[pltpu.VMEM((seq + HIST_PAD, D_CONV), F32)],
        compiler_params=_params(("parallel",)),
        name="conv",
    )(u, hist, w, b, g, nbias, bd)


def _outproj_kernel(x_ref, a_ref, m_ref, c_ref, w_ref, o_ref):
    acc = jnp.dot(a_ref[...].astype(BF16), w_ref[0:D_ATTN, :], preferred_element_type=F32)
    acc += jnp.dot(m_ref[...].astype(BF16), w_ref[D_ATTN:D_ATTN + D_MLSTM, :], preferred_element_type=F32)
    acc += jnp.dot(c_ref[...].astype(BF16), w_ref[D_ATTN + D_MLSTM:, :], preferred_element_type=F32)
    o_ref[...] = x_ref[...] + acc


def _outproj(x, att, mo, cv, w_out):
    T = x.shape[0]
    tm = _tile(T, 640, LANES)
    row = lambda i: (i, 0)
    return pl.pallas_call(
        _outproj_kernel,
        grid=(T // tm,),
        in_specs=[pl.BlockSpec((tm, D_MODEL), row), pl.BlockSpec((tm, D_ATTN), row),
                  pl.BlockSpec((tm, D_MLSTM), row), pl.BlockSpec((tm, D_CONV), row),
                  pl.BlockSpec((D_MODEL, D_MODEL), lambda i: (0, 0))],
        out_specs=pl.BlockSpec((tm, D_MODEL), row),
        out_shape=jax.ShapeDtypeStruct((T, D_MODEL), F32),
        compiler_params=_params(("parallel",)),
        name="outproj",
    )(x, att, mo, cv, w_out)


def _ffn_kernel(x_ref, g_ref, w1_ref, w3_ref, w2_ref, o_ref, h_s, acc_s):
    j = pl.program_id(1)

    @pl.when(j == 0)
    def _():
        x = x_ref[...]
        h_s[...] = (x * lax.rsqrt(jnp.mean(x * x, axis=-1, keepdims=True) + EPS) * g_ref[...]).astype(BF16)
        acc_s[...] = x

    h = h_s[...]
    a = jnp.dot(h, w1_ref[...], preferred_element_type=F32)
    g = jnp.dot(h, w3_ref[...], preferred_element_type=F32)
    act = (a * jax.nn.sigmoid(a) * g).astype(BF16)
    acc_s[...] += jnp.dot(act, w2_ref[...], preferred_element_type=F32)

    @pl.when(j == pl.num_programs(1) - 1)
    def _():
        o_ref[...] = acc_s[...]


def _ffn(x, g, w1, w3, w2):
    T = x.shape[0]
    tm = _tile(T, 640, LANES)
    tf = D_FF // 2
    row = lambda i, j: (i, 0)
    return pl.pallas_call(
        _ffn_kernel,
        grid=(T // tm, D_FF // tf),
        in_specs=[pl.BlockSpec((tm, D_MODEL), row), pl.BlockSpec((1, D_MODEL), lambda i, j: (0, 0)),
                  pl.BlockSpec((D_MODEL, tf), lambda i, j: (0, j)),
                  pl.BlockSpec((D_MODEL, tf), lambda i, j: (0, j)),
                  pl.BlockSpec((tf, D_MODEL), lambda i, j: (j, 0))],
        out_specs=pl.BlockSpec((tm, D_MODEL), row),
        out_shape=jax.ShapeDtypeStruct((T, D_MODEL), F32),
        scratch_shapes=[pltpu.VMEM((tm, D_MODEL), BF16), pltpu.VMEM((tm, D_MODEL), F32)],
        compiler_params=_params(("parallel", "arbitrary")),
        name="ffn",
    )(x, g, w1, w3, w2)


def _router_kernel(x_ref, g_ref, r_ref, comb_ref):
    x = x_ref[...]
    h = x * lax.rsqrt(jnp.mean(x * x, axis=-1, keepdims=True) + EPS) * g_ref[...]
    logits = jnp.dot(h, r_ref[...], preferred_element_type=F32, precision=lax.Precision.HIGHEST)
    lane = lax.broadcasted_iota(jnp.int32, logits.shape, 1)
    logits = jnp.where(lane < N_EXPERTS, logits, NEG)
    v1 = jnp.max(logits, axis=1, keepdims=True)
    i1 = jnp.min(jnp.where(logits == v1, lane, LANES), axis=1, keepdims=True)
    rest = jnp.where(lane == i1, NEG, logits)
    v2 = jnp.max(rest, axis=1, keepdims=True)
    i2 = jnp.min(jnp.where(rest == v2, lane, LANES), axis=1, keepdims=True)
    e2 = jnp.exp(v2 - v1)
    inv = 1.0 / (1.0 + e2)
    comb_ref[...] = jnp.where(lane == i1, inv, 0.0) + jnp.where(lane == i2, e2 * inv, 0.0)


def _router(x, g, r_pad):
    T = x.shape[0]
    tm = _tile(T, 640, LANES)
    row = lambda i: (i, 0)
    return pl.pallas_call(
        _router_kernel,
        grid=(T // tm,),
        in_specs=[pl.BlockSpec((tm, D_MODEL), row), pl.BlockSpec((1, D_MODEL), lambda i: (0, 0)),
                  pl.BlockSpec((D_MODEL, LANES), lambda i: (0, 0))],
        out_specs=pl.BlockSpec((tm, LANES), row),
        out_shape=jax.ShapeDtypeStruct((T, LANES), F32),
        compiler_params=_params(("parallel",)),
        name="router",
    )(x, g, r_pad)


def _moe_kernel(x_ref, g_ref, comb_ref, w1_ref, w3_ref, w2_ref, o_ref, h_s, acc_s):
    e = pl.program_id(1)
    j = pl.program_id(2)

    @pl.when((e == 0) & (j == 0))
    def _():
        x = x_ref[...]
        h_s[...] = (x * lax.rsqrt(jnp.mean(x * x, axis=-1, keepdims=True) + EPS) * g_ref[...]).astype(BF16)
        acc_s[...] = x

    lane = lax.broadcasted_iota(jnp.int32, (1, LANES), 1)
    gate = jnp.sum(jnp.where(lane == e, comb_ref[...], 0.0), axis=1, keepdims=True)
    h = h_s[...]
    a = jnp.dot(h, w1_ref[...].astype(BF16), preferred_element_type=F32)
    g = jnp.dot(h, w3_ref[...].astype(BF16), preferred_element_type=F32)
    act = (a * jax.nn.sigmoid(a) * g).astype(BF16)
    acc_s[...] += gate * jnp.dot(act, w2_ref[...].astype(BF16), preferred_element_type=F32)

    @pl.when((e == pl.num_programs(1) - 1) & (j == pl.num_programs(2) - 1))
    def _():
        o_ref[...] = acc_s[...]


def _moe(x, g, comb, w1, w3, w2):
    T = x.shape[0]
    tm = _tile(T, 1280, LANES)
    tf = 256
    row = lambda i, e, j: (i, 0)
    return pl.pallas_call(
        _moe_kernel,
        grid=(T // tm, N_EXPERTS, D_FF // tf),
        in_specs=[pl.BlockSpec((tm, D_MODEL), row), pl.BlockSpec((1, D_MODEL), lambda i, e, j: (0, 0)),
                  pl.BlockSpec((tm, LANES), row),
                  pl.BlockSpec((None, D_MODEL, tf), lambda i, e, j: (e, 0, j)),
                  pl.BlockSpec((None, D_MODEL, tf), lambda i, e, j: (e, 0, j)),
                  pl.BlockSpec((None, tf, D_MODEL), lambda i, e, j: (e, j, 0))],
        out_specs=pl.BlockSpec((tm, D_MODEL), row),
        out_shape=jax.ShapeDtypeStruct((T, D_MODEL), F32),
        scratch_shapes=[pltpu.VMEM((tm, D_MODEL), BF16), pltpu.VMEM((tm, D_MODEL), F32)],
        compiler_params=_params(("parallel", "arbitrary", "arbitrary")),
        name="moe",
    )(x, g, comb, w1, w3, w2)


def _rope_tables(pos):
    half = ROPE_DIM // 2
    inv_freq = ROPE_THETA ** (-2.0 * jnp.arange(half, dtype=F32) / ROPE_DIM)
    ang = pos[:, None] * inv_freq[None, :]
    cos, sin = jnp.cos(ang), jnp.sin(ang)
    n = pos.shape[0]
    ones = jnp.ones((n, HEAD_DIM - ROPE_DIM), F32)
    zeros = jnp.zeros((n, HEAD_DIM - ROPE_DIM), F32)
    zh = jnp.zeros((n, half), F32)
    c = jnp.concatenate([cos, cos, ones], axis=1)
    s1 = jnp.concatenate([zh, sin, zeros], axis=1)
    s2 = jnp.concatenate([-sin, zh, zeros], axis=1)
    two = lambda t: jnp.concatenate([t, t], axis=1)
    return two(c), two(s1), two(s2)


def _pair_state(c, n, m):
    nb = c.shape[0]
    c = c.reshape(nb, 2, 2, HEAD_DIM, HEAD_DIM)
    z = jnp.zeros((nb, 2, HEAD_DIM, HEAD_DIM), F32)
    top = jnp.concatenate([c[:, :, 0], z], axis=-1)
    bot = jnp.concatenate([z, c[:, :, 1]], axis=-1)
    cp = jnp.concatenate([top, bot], axis=-2)
    nm = jnp.zeros((nb, 8, LANES), F32)
    nm = nm.at[:, 0:2, :].set(n.reshape(nb, 2, LANES))
    nm = nm.at[:, 2, 0:N_HEADS_MLSTM].set(m)
    return cp, nm


def _unpair_state(cp, nm):
    nb = cp.shape[0]
    c = jnp.stack([cp[:, :, :HEAD_DIM, :HEAD_DIM], cp[:, :, HEAD_DIM:, HEAD_DIM:]], axis=2)
    c = c.reshape(nb, N_HEADS_MLSTM, HEAD_DIM, HEAD_DIM)
    n = nm[:, 0:2, :].reshape(nb, N_HEADS_MLSTM, HEAD_DIM)
    m = nm[:, 2, 0:N_HEADS_MLSTM]
    return c, n, m


def kernel(x_prompt, x_sample, cache_attn_k, cache_attn_v, state_mlstm_C, state_mlstm_n, state_mlstm_m, state_conv, norm_mix, w_in, q_norm, k_norm, mlstm_gate_bias, mlstm_out_norm, conv_dw_w, conv_dw_b, conv_norm_g, conv_norm_b, w_out, norm_ffn, ffn_w1, ffn_w3, ffn_w2, moe_router, moe_w1, moe_w3, moe_w2):
    batch, seq, _ = x_prompt.shape
    nbs, dec, _ = x_sample.shape
    depth = w_in.shape[0]
    tp = batch * seq
    keep = min(max(w for w, _ in DILATED_BRANCHES), seq)
    hist_rows = CONV_WIDTH - 1

    x = jnp.concatenate([x_prompt.reshape(tp, D_MODEL), x_sample.reshape(nbs * dec, D_MODEL)], axis=0)

    pos = jnp.concatenate([jnp.tile(jnp.arange(seq, dtype=F32), batch),
                           jnp.tile(PAST_LEN + jnp.arange(dec, dtype=F32), nbs)])
    rc, rs1, rs2 = _rope_tables(pos)
    gi = np.arange(D_ATTN) // HEAD_DIM
    bd = jnp.asarray((gi[:, None] == gi[None, :]).astype(np.float32) / HEAD_DIM, dtype=BF16)

    p_len = cache_attn_k.shape[2]
    cache_k = cache_attn_k.reshape(depth, nbs, p_len, D_ATTN)
    cache_v = cache_attn_v.reshape(depth, nbs, p_len, D_ATTN)

    zero_c = jnp.zeros((batch, 2, LANES, LANES), F32)
    zero_nm = jnp.zeros((batch, 8, LANES), F32)
    zero_hist = jnp.zeros((batch, HIST_PAD, D_CONV), F32)

    outs = [[] for _ in range(12)]
    for l in range(depth):
        o = 3 * D_ATTN + 4 * D_MLSTM
        w_p = jnp.concatenate([w_in[l][:, :o], jnp.pad(w_in[l][:, o:o + 8], ((0, 0), (0, GATE_PAD - 8))),
                               w_in[l][:, o + 8:]], axis=1).astype(BF16)
        qn_row = jnp.tile(q_norm[l], N_HEADS_ATTN)[None, :]
        kn_row = jnp.tile(k_norm[l], N_HEADS_ATTN)[None, :]
        gb_row = jnp.pad(mlstm_gate_bias[l], (0, GATE_PAD - 2 * N_HEADS_MLSTM))[None, :]

        q, k, v, zm, gt, u = _inproj(x, norm_mix[l][None, :], w_p, qn_row, kn_row, gb_row, bd, rc, rs1, rs2)

        att_p = _attn_prompt(q, k, v, batch, seq)
        on_row = mlstm_out_norm[l][None, :]
        hm_p, c1_p, nm1_p = _mlstm(zm, gt, zero_c, zero_nm, on_row, batch, seq)
        w_conv = jnp.pad(conv_dw_w[l], ((0, HIST_PAD - CONV_WIDTH), (0, 0)))
        conv_args = (w_conv, conv_dw_b[l][None, :], conv_norm_g[l][None, :], conv_norm_b[l][None, :], bd)
        cv_p, st_p = _conv(u, zero_hist, *conv_args, batch, seq)

        sq = lambda t: t[tp:].reshape(nbs, dec, t.shape[1])
        att_s = _attn_sample(sq(q), sq(k), sq(v), cache_k, cache_v, l)
        c0_s, nm0_s = _pair_state(state_mlstm_C[l], state_mlstm_n[l], state_mlstm_m[l])
        hm_s, c1_s, nm1_s = _mlstm(zm[tp:], gt[tp:], c0_s, nm0_s, on_row, nbs, dec)
        hist_s = jnp.pad(state_conv[l], ((0, 0), (HIST_PAD - hist_rows, 0), (0, 0)))
        cv_s, st_s = _conv(u[tp:], hist_s, *conv_args, nbs, dec)

        att = jnp.concatenate([att_p, att_s.reshape(nbs * dec, D_ATTN)], axis=0)
        hm = jnp.concatenate([hm_p, hm_s], axis=0)
        cv = jnp.concatenate([cv_p, cv_s], axis=0)
        x = _outproj(x, att, hm, cv, w_out[l].astype(BF16))

        i = l // 2
        if l % 2 == 0:
            x = _ffn(x, norm_ffn[l][None, :], ffn_w1[i].astype(BF16), ffn_w3[i].astype(BF16),
                     ffn_w2[i].astype(BF16))
        else:
            r_pad = jnp.pad(moe_router[i], ((0, 0), (0, LANES - N_EXPERTS)))
            comb = _router(x, norm_ffn[l][None, :], r_pad)
            x = _moe(x, norm_ffn[l][None, :], comb, moe_w1[i], moe_w3[i], moe_w2[i])

        kp = k[:tp].reshape(batch, seq, N_HEADS_ATTN, HEAD_DIM)[:, seq - keep:]
        vp = v[:tp].reshape(batch, seq, N_HEADS_ATTN, HEAD_DIM)[:, seq - keep:]
        cp_, np_, mp_ = _unpair_state(c1_p, nm1_p)
        cs_, ns_, ms_ = _unpair_state(c1_s, nm1_s)
        layer_outs = (kp, vp, sq(k).reshape(nbs, dec, N_HEADS_ATTN, HEAD_DIM),
                      sq(v).reshape(nbs, dec, N_HEADS_ATTN, HEAD_DIM),
                      cp_, np_, mp_, cs_, ns_, ms_,
                      st_p[:, HIST_PAD - hist_rows:], st_s[:, HIST_PAD - hist_rows:])
        for lst, val in zip(outs, layer_outs):
            lst.append(val)

    y_prompt = x[:tp].reshape(batch, seq, D_MODEL)
    y_sample = x[tp:].reshape(nbs, dec, D_MODEL)
    return (y_prompt, y_sample) + tuple(jnp.stack(lst, axis=0) for lst in outs)
```

```python
import functools

import numpy as np
import jax
import jax.numpy as jnp
from jax import lax
from jax.experimental import pallas as pl
from jax.experimental.pallas import tpu as pltpu

F32 = jnp.float32
BF16 = jnp.bfloat16

D_MODEL = 1024
HEAD_DIM = 64
N_HEADS_ATTN = 6
N_HEADS_MLSTM = 4
D_ATTN = N_HEADS_ATTN * HEAD_DIM
D_MLSTM = N_HEADS_MLSTM * HEAD_DIM
D_CONV = 384
DILATED_BRANCHES = ((128, 1), (512, 4), (2048, 16))
ATTN_BLOCK = 128
ROPE_DIM = HEAD_DIM // 4
ROPE_THETA = 500000.0
MLSTM_CHUNK = 128
CONV_WIDTH = 31
D_FF = 2816
N_EXPERTS = 8
EPS = 1e-6
NEG = -1e30
PAST_LEN = 16384

LANES = 128
HIST_PAD = 32
GATE_PAD = LANES
W_IN_COLS = 3 * D_ATTN + 4 * D_MLSTM + GATE_PAD + 2 * D_CONV
VMEM_LIMIT = 56 * 1024 * 1024
MOE_TILE = 512
MOE_FF_CHUNK = 256
ROUTE_I1, ROUTE_I2, ROUTE_G1, ROUTE_G2 = 8, 9, 10, 11


def _tile(n, pref, mult=8):
    for t in range(min(pref, n), 0, -1):
        if n % t == 0 and t % mult == 0:
            return t
    return n


def _params(sem):
    return pltpu.CompilerParams(dimension_semantics=sem, vmem_limit_bytes=VMEM_LIMIT)


def _split_dot(a, b_bf16):
    hi = a.astype(BF16)
    lo = (a - hi.astype(F32)).astype(BF16)
    return (jnp.dot(hi, b_bf16, preferred_element_type=F32)
            + jnp.dot(lo, b_bf16, preferred_element_type=F32))


def _group_mean(a, bd_ref):
    return _split_dot(a, bd_ref[...])


def _inproj_kernel(x_ref, g_ref, w_ref, qn_ref, kn_ref, gb_ref, bd_ref, rc_ref, rs1_ref, rs2_ref,
                   q_ref, k_ref, v_ref, zm_ref, gt_ref, u_ref):
    x = x_ref[...]
    h = (x * lax.rsqrt(jnp.mean(x * x, axis=-1, keepdims=True) + EPS) * g_ref[...]).astype(BF16)

    def proj(lo, hi):
        return jnp.dot(h, w_ref[:, lo:hi], preferred_element_type=F32)

    rc, rs1, rs2 = rc_ref[...], rs1_ref[...], rs2_ref[...]

    def norm_rope(z, gain_ref, scale):
        zn = z * lax.rsqrt(_group_mean(z * z, bd_ref) + EPS) * gain_ref[...]
        outs = []
        for c in range(D_ATTN // LANES):
            zc = zn[:, c * LANES:(c + 1) * LANES]
            y = (zc * rc + pltpu.roll(zc, ROPE_DIM // 2, 1) * rs1
                 + pltpu.roll(zc, LANES - ROPE_DIM // 2, 1) * rs2)
            outs.append(y * scale if scale != 1.0 else y)
        return jnp.concatenate(outs, axis=1)

    q_ref[...] = norm_rope(proj(0, D_ATTN), qn_ref, HEAD_DIM ** -0.5)
    k_ref[...] = norm_rope(proj(D_ATTN, 2 * D_ATTN), kn_ref, 1.0)
    v_ref[...] = proj(2 * D_ATTN, 3 * D_ATTN)

    o = 3 * D_ATTN
    zm = proj(o, o + 4 * D_MLSTM)
    lane = lax.broadcasted_iota(jnp.int32, (1, 4 * D_MLSTM), 1)
    zm_ref[...] = zm * jnp.where((lane >= D_MLSTM) & (lane < 2 * D_MLSTM), HEAD_DIM ** -0.5, 1.0)

    o += 4 * D_MLSTM
    gt = proj(o, o + GATE_PAD) + gb_ref[...]
    glane = lax.broadcasted_iota(jnp.int32, (1, GATE_PAD), 1)
    log_sig = jnp.minimum(gt, 0.0) - jnp.log1p(jnp.exp(-jnp.abs(gt)))
    gt_ref[...] = jnp.where(glane < N_HEADS_MLSTM, gt, log_sig)

    o += GATE_PAD
    cv = proj(o, o + D_CONV)
    cg = proj(o + D_CONV, o + 2 * D_CONV)
    u_ref[...] = cv * jax.nn.sigmoid(cg)


def _inproj(x, g, w_p, qn_row, kn_row, gb_row, bd, rc, rs1, rs2):
    T = x.shape[0]
    tm = _tile(T, 640, LANES)
    row = lambda i: (i, 0)
    const = lambda i: (0, 0)
    out_shapes = [jax.ShapeDtypeStruct((T, n), F32)
                  for n in (D_ATTN, D_ATTN, D_ATTN, 4 * D_MLSTM, GATE_PAD, D_CONV)]
    return pl.pallas_call(
        _inproj_kernel,
        grid=(T // tm,),
        in_specs=[pl.BlockSpec((tm, D_MODEL), row),
                  pl.BlockSpec((1, D_MODEL), const),
                  pl.BlockSpec((D_MODEL, W_IN_COLS), const),
                  pl.BlockSpec((1, D_ATTN), const),
                  pl.BlockSpec((1, D_ATTN), const),
                  pl.BlockSpec((1, GATE_PAD), const),
                  pl.BlockSpec((D_ATTN, D_ATTN), const),
                  pl.BlockSpec((tm, LANES), row),
                  pl.BlockSpec((tm, LANES), row),
                  pl.BlockSpec((tm, LANES), row)],
        out_specs=[pl.BlockSpec((tm, s.shape[1]), row) for s in out_shapes],
        out_shape=out_shapes,
        compiler_params=_params(("parallel",)),
        name="inproj",
    )(x, g, w_p, qn_row, kn_row, gb_row, bd, rc, rs1, rs2)


def _attn_prompt_kernel(q_ref, k_ref, v_ref, o_ref, qs, ks, vs, racc, rm, rden, nacc, nm, nden, *, seq):
    blk = ATTN_BLOCK
    n_blocks = seq // blk
    lane = lax.broadcasted_iota(jnp.int32, (1, LANES), 1)
    head_a = lane < HEAD_DIM
    row = lax.broadcasted_iota(jnp.int32, (blk, 2 * blk), 0)
    col = lax.broadcasted_iota(jnp.int32, (blk, 2 * blk), 1)
    band = (col >= row) & (col <= row + blk)
    bias_band = jnp.where(band, 0.0, NEG)
    bias_first = jnp.where(band & (col >= blk), 0.0, NEG)

    ks[0:blk, :] = jnp.zeros((blk, LANES), BF16)
    vs[0:blk, :] = jnp.zeros((blk, LANES), BF16)

    cp = 256
    for (_, dil) in DILATED_BRANCHES:
        nsub = seq // dil
        nb = nsub // blk
        for r in range(dil):
            for c0 in range(0, nsub, cp):
                n = min(cp, nsub - c0)
                if dil == 1:
                    src = pl.ds(c0, n)
                else:
                    src = pl.ds(r + c0 * dil, n, stride=dil)
                dst = r * nsub + c0
                qs[dst:dst + n, :] = q_ref[src, :].astype(BF16)
                ks[blk + dst:blk + dst + n, :] = k_ref[src, :].astype(BF16)
                vs[blk + dst:blk + dst + n, :] = v_ref[src, :].astype(BF16)

        if dil == 1:
            dst_acc, dst_m, dst_den = nacc, nm, nden
        else:
            dst_acc, dst_m, dst_den = racc, rm, rden

        def body(j, carry, nb=nb, dst_acc=dst_acc, dst_m=dst_m, dst_den=dst_den):
            off = pl.multiple_of(j * blk, blk)
            qb = qs[pl.ds(off, blk), :]
            zero = jnp.zeros_like(qb)
            q2 = jnp.concatenate([jnp.where(head_a, qb, zero), jnp.where(head_a, zero, qb)], axis=0)
            kk = ks[pl.ds(off, 2 * blk), :]
            vv = vs[pl.ds(off, 2 * blk), :]
            s = lax.dot_general(q2, kk, (((1,), (1,)), ((), ())), preferred_element_type=F32)
            bias = jnp.where(j % nb == 0, bias_first, bias_band)
            s = s + jnp.concatenate([bias, bias], axis=0)
            mx = jnp.max(s, axis=1, keepdims=True)
            p = jnp.exp(s - mx)
            den = jnp.sum(p, axis=1, keepdims=True)
            o = jnp.dot(p.astype(BF16), vv, preferred_element_type=F32)
            dst_acc[pl.ds(off, blk), :] = jnp.where(head_a, o[:blk], o[blk:])
            dst_m[pl.ds(off, blk), :] = jnp.where(head_a, mx[:blk], mx[blk:])
            dst_den[pl.ds(off, blk), :] = jnp.where(head_a, den[:blk], den[blk:])
            return carry

        lax.fori_loop(0, n_blocks, body, 0)

        if dil > 1:
            for r in range(dil):
                for c0 in range(0, nsub, cp):
                    n = min(cp, nsub - c0)
                    nat = pl.ds(r + c0 * dil, n, stride=dil)
                    res = pl.ds(r * nsub + c0, n)
                    m_old, m_new = nm[nat, :], rm[res, :]
                    m_all = jnp.maximum(m_old, m_new)
                    e_old = jnp.exp(m_old - m_all)
                    e_new = jnp.exp(m_new - m_all)
                    nacc[nat, :] = e_old * nacc[nat, :] + e_new * racc[res, :]
                    nden[nat, :] = e_old * nden[nat, :] + e_new * rden[res, :]
                    nm[nat, :] = m_all

    o_ref[...] = nacc[...] / nden[...]


def _attn_prompt(q, k, v, batch, seq):
    n_pairs = D_ATTN // LANES
    spec = pl.BlockSpec((seq, LANES), lambda b, hp: (b, hp))
    return pl.pallas_call(
        functools.partial(_attn_prompt_kernel, seq=seq),
        grid=(batch, n_pairs),
        in_specs=[spec, spec, spec],
        out_specs=spec,
        out_shape=jax.ShapeDtypeStruct((batch * seq, D_ATTN), F32),
        scratch_shapes=[pltpu.VMEM((seq, LANES), BF16),
                        pltpu.VMEM((seq + ATTN_BLOCK, LANES), BF16),
                        pltpu.VMEM((seq + ATTN_BLOCK, LANES), BF16)]
                       + [pltpu.VMEM((seq, LANES), F32) for _ in range(6)],
        compiler_params=_params(("parallel", "parallel")),
        name="attn_prompt",
    )(q, k, v)


def _attn_sample_kernel(q_ref, kn_ref, vn_ref, kc_ref, vc_ref, cw_ref, o_ref, kall, vall, *, p_len, dec):
    pad_rows = kall.shape[0] - p_len
    kall[0:p_len, :] = kc_ref[...].astype(BF16)
    vall[0:p_len, :] = vc_ref[...].astype(BF16)
    tail = jnp.zeros((pad_rows - dec, D_ATTN), F32)
    kall[p_len:, :] = jnp.concatenate([kn_ref[...], tail], axis=0).astype(BF16)
    vall[p_len:, :] = jnp.concatenate([vn_ref[...], tail], axis=0).astype(BF16)

    lane = lax.broadcasted_iota(jnp.int32, (1, LANES), 1)
    head_a = lane < HEAD_DIM
    cw = cw_ref[...]
    cw2 = jnp.concatenate([cw, cw], axis=0)
    outs = []
    for hp in range(D_ATTN // LANES):
        sl = slice(hp * LANES, (hp + 1) * LANES)
        qb = q_ref[:, sl].astype(BF16)
        zero = jnp.zeros_like(qb)
        q2 = jnp.concatenate([jnp.where(head_a, qb, zero), jnp.where(head_a, zero, qb)], axis=0)
        s = lax.dot_general(q2, kall[:, sl], (((1,), (1,)), ((), ())), preferred_element_type=F32)
        s = jnp.where(cw2 > 0.0, s, NEG)
        mx = jnp.max(s, axis=1, keepdims=True)
        p = jnp.exp(s - mx) * cw2
        den = jnp.sum(p, axis=1, keepdims=True)
        o = jnp.dot(p.astype(BF16), vall[:, sl], preferred_element_type=F32) / den
        outs.append(jnp.where(head_a, o[:dec], o[dec:]))
    o_ref[...] = jnp.concatenate(outs, axis=1)


def _sample_multiplicity(p_len, dec, n_rows):
    t = np.arange(dec)[:, None]
    i = np.arange(n_rows)[None, :]
    dist = p_len + t - i
    cw = np.zeros((dec, n_rows), np.float32)
    for (w, d) in DILATED_BRANCHES:
        cw += ((dist >= 0) & (dist <= w) & (dist % d == 0) & (i < p_len + dec))
    return jnp.asarray(cw)


def _attn_sample(q, k, v, cache_k, cache_v, layer):
    nb, dec, _ = q.shape
    p_len = cache_k.shape[2]
    n_rows = p_len + LANES
    cw = _sample_multiplicity(p_len, dec, n_rows)
    new = pl.BlockSpec((None, dec, D_ATTN), lambda b: (b, 0, 0))
    cache = pl.BlockSpec((None, None, p_len, D_ATTN), lambda b: (layer, b, 0, 0))
    return pl.pallas_call(
        functools.partial(_attn_sample_kernel, p_len=p_len, dec=dec),
        grid=(nb,),
        in_specs=[new, new, new, cache, cache, pl.BlockSpec((dec, n_rows), lambda b: (0, 0))],
        out_specs=new,
        out_shape=jax.ShapeDtypeStruct((nb, dec, D_ATTN), F32),
        scratch_shapes=[pltpu.VMEM((n_rows, D_ATTN), BF16), pltpu.VMEM((n_rows, D_ATTN), BF16)],
        compiler_params=_params(("parallel",)),
        name="attn_sample",
    )(q, k, v, cache_k, cache_v, cw)


def _mlstm_kernel(zm_ref, gt_ref, c0_ref, nm0_ref, on_ref, h_ref, c1_ref, nm1_ref, c_s, nm_s, *, rows):
    ch = MLSTM_CHUNK
    step = pl.program_id(1)

    @pl.when(step == 0)
    def _():
        c_s[...] = c0_ref[...]
        nm_s[...] = nm0_ref[...]

    def padded(a):
        if rows == ch:
            return a
        return jnp.concatenate([a, jnp.zeros((ch - rows, a.shape[1]), a.dtype)], axis=0)

    lane = lax.broadcasted_iota(jnp.int32, (1, LANES), 1)
    head_a = lane < HEAD_DIM
    ri = lax.broadcasted_iota(jnp.int32, (ch, ch), 0)
    ci = lax.broadcasted_iota(jnp.int32, (ch, ch), 1)
    causal = ci <= ri
    eye = ci == ri
    rowv = lax.broadcasted_iota(jnp.int32, (ch, 1), 0)
    valid = rowv < rows

    gt = padded(gt_ref[...])
    glane = lax.broadcasted_iota(jnp.int32, (1, LANES), 1)
    is_ig = glane < N_HEADS_MLSTM
    gt = jnp.where(valid, gt, jnp.where(is_ig, NEG, 0.0))
    lf_only = jnp.where(is_ig, 0.0, gt)
    fcum = jnp.dot(causal.astype(F32), lf_only, preferred_element_type=F32,
                   precision=lax.Precision.HIGHEST)

    nm = nm_s[...]
    m_row_new = nm[2:3, :]
    brow128 = lax.broadcasted_iota(jnp.int32, (LANES, 1), 0) < HEAD_DIM
    bd_mask = brow128 == head_a

    for pair in range(N_HEADS_MLSTM // 2):
        ls = slice(pair * LANES, (pair + 1) * LANES)
        qp = padded(zm_ref[:, ls])
        kp = padded(zm_ref[:, 2 * LANES + pair * LANES:2 * LANES + (pair + 1) * LANES])
        vp = padded(zm_ref[:, 4 * LANES + pair * LANES:4 * LANES + (pair + 1) * LANES])
        op = padded(zm_ref[:, 6 * LANES + pair * LANES:6 * LANES + (pair + 1) * LANES])
        kb, vb = kp.astype(BF16), vp.astype(BF16)
        c_pair = c_s[pair]
        cb = c_pair.astype(BF16)
        n_row = nm[pair:pair + 1, :]

        h_heads, wl_heads, decays = [], [], []
        for sub in range(2):
            hx = 2 * pair + sub
            mask = head_a if sub == 0 else jnp.logical_not(head_a)
            a = fcum[:, N_HEADS_MLSTM + hx:N_HEADS_MLSTM + hx + 1]
            igc = gt[:, hx:hx + 1]
            m0 = nm[2:3, hx:hx + 1]
            brow = jnp.sum(jnp.where(eye, igc - a, 0.0), axis=0, keepdims=True)
            dmat = jnp.where(causal, a + brow, NEG)
            inter = a + m0
            m = jnp.maximum(inter, jnp.max(dmat, axis=1, keepdims=True))
            w = jnp.exp(dmat - m)
            g = jnp.exp(inter - m)
            qx = jnp.where(mask, qp, 0.0)
            qxb = qx.astype(BF16)
            sc = lax.dot_general(qxb, kb, (((1,), (1,)), ((), ())), preferred_element_type=F32) * w
            num = (jnp.dot(sc.astype(BF16), vb, preferred_element_type=F32)
                   + g * jnp.dot(qxb, cb, preferred_element_type=F32))
            den = (jnp.sum(sc, axis=1, keepdims=True)
                   + g * jnp.sum(qx * n_row, axis=1, keepdims=True))
            h_heads.append(num / jnp.maximum(jnp.abs(den), jnp.exp(-m)))
            m_last = m[rows - 1:rows, :]
            f_last = a[rows - 1:rows, :]
            decays.append(jnp.exp(f_last + m0 - m_last))
            wl_heads.append(jnp.exp(f_last - a + igc - m_last))
            m_row_new = jnp.where(glane == hx, m_last, m_row_new)

        kw = kp * jnp.where(head_a, wl_heads[0], wl_heads[1])
        upd = lax.dot_general(kw.astype(BF16), vb, (((0,), (0,)), ((), ())), preferred_element_type=F32)
        c_s[pair] = jnp.where(bd_mask, jnp.where(brow128, decays[0], decays[1]) * c_pair + upd, 0.0)
        n_new = jnp.where(head_a, decays[0], decays[1]) * n_row + jnp.sum(kw, axis=0, keepdims=True)
        nm_s[pair:pair + 1, :] = n_new

        y = jax.nn.sigmoid(op) * jnp.where(head_a, h_heads[0], h_heads[1])
        y2 = y * y
        ss_a = jnp.sum(jnp.where(head_a, y2, 0.0), axis=1, keepdims=True)
        ss_b = jnp.sum(jnp.where(head_a, 0.0, y2), axis=1, keepdims=True)
        ms = jnp.where(head_a, ss_a, ss_b) * (1.0 / HEAD_DIM)
        out = y * lax.rsqrt(ms + EPS) * on_ref[:, ls]
        h_ref[:, ls] = out[:rows]

    nm_s[2:3, :] = m_row_new

    @pl.when(step == pl.num_programs(1) - 1)
    def _():
        c1_ref[...] = c_s[...]
        nm1_ref[...] = nm_s[...]


def _mlstm(zm, gt, c0, nm0, on_row, nb, seq):
    rows = min(seq, MLSTM_CHUNK)
    n_chunks = seq // rows
    tok = lambda b, c: (b * n_chunks + c, 0)
    state = lambda b, c: (b, 0, 0, 0)
    return pl.pallas_call(
        functools.partial(_mlstm_kernel, rows=rows),
        grid=(nb, n_chunks),
        in_specs=[pl.BlockSpec((rows, 4 * D_MLSTM), tok),
                  pl.BlockSpec((rows, GATE_PAD), tok),
                  pl.BlockSpec((None, 2, LANES, LANES), state),
                  pl.BlockSpec((None, 8, LANES), lambda b, c: (b, 0, 0)),
                  pl.BlockSpec((1, D_MLSTM), lambda b, c: (0, 0))],
        out_specs=[pl.BlockSpec((rows, D_MLSTM), tok),
                   pl.BlockSpec((None, 2, LANES, LANES), state),
                   pl.BlockSpec((None, 8, LANES), lambda b, c: (b, 0, 0))],
        out_shape=[jax.ShapeDtypeStruct((nb * seq, D_MLSTM), F32),
                   jax.ShapeDtypeStruct((nb, 2, LANES, LANES), F32),
                   jax.ShapeDtypeStruct((nb, 8, LANES), F32)],
        scratch_shapes=[pltpu.VMEM((2, LANES, LANES), F32), pltpu.VMEM((8, LANES), F32)],
        compiler_params=_params(("parallel", "arbitrary")),
        name="mlstm",
    )(zm, gt, c0, nm0, on_row)


def _conv_kernel(u_ref, hist_ref, w_ref, b_ref, g_ref, nb_ref, bd_ref, c_ref, st_ref, uc, *, seq, rt):
    lead = HIST_PAD - (CONV_WIDTH - 1)
    uc[0:HIST_PAD, :] = hist_ref[...]
    uc[HIST_PAD:, :] = u_ref[...]
    w = w_ref[...]

    def chunk(c, carry):
        base = pl.multiple_of(c * rt, rt)
        win = uc[pl.ds(base, rt + HIST_PAD), :]
        acc = jnp.zeros((rt, D_CONV), F32)
        for j in range(CONV_WIDTH):
            acc = acc + win[lead + j:lead + j + rt, :] * w[j:j + 1, :]
        y = acc + b_ref[...]
        mu = _group_mean(y, bd_ref)
        yc = y - mu
        var = _group_mean(yc * yc, bd_ref)
        z = yc * lax.rsqrt(var + EPS) * g_ref[...] + nb_ref[...]
        c_ref[pl.ds(base, rt), :] = z * jax.nn.sigmoid(z)
        return carry

    lax.fori_loop(0, seq // rt, chunk, 0)
    st_ref[...] = uc[seq:seq + HIST_PAD, :]


def _conv(u, hist, w, b, g, nbias, bd, nb, seq):
    rt = min(seq, 256)
    tok = lambda i: (i, 0)
    const = lambda i: (0, 0)
    hspec = pl.BlockSpec((None, HIST_PAD, D_CONV), lambda i: (i, 0, 0))
    return pl.pallas_call(
        functools.partial(_conv_kernel, seq=seq, rt=rt),
        grid=(nb,),
        in_specs=[pl.BlockSpec((seq, D_CONV), tok), hspec,
                  pl.BlockSpec((HIST_PAD, D_CONV), const),
                  pl.BlockSpec((1, D_CONV), const), pl.BlockSpec((1, D_CONV), const),
                  pl.BlockSpec((1, D_CONV), const), pl.BlockSpec((D_CONV, D_CONV), const)],
        out_specs=[pl.BlockSpec((seq, D_CONV), tok), hspec],
        out_shape=[jax.ShapeDtypeStruct((nb * seq, D_CONV), F32),
                   jax.ShapeDtypeStruct((nb, HIST_PAD, D_CONV), F32)],
        scratch_shapes=[pltpu.VMEM((seq + HIST_PAD, D_CONV), F32)],
        compiler_params=_params(("parallel",)),
        name="conv",
    )(u, hist, w, b, g, nbias, bd)


def _outproj_kernel(x_ref, a_ref, m_ref, c_ref, w_ref, o_ref):
    acc = jnp.dot(a_ref[...].astype(BF16), w_ref[0:D_ATTN, :], preferred_element_type=F32)
    acc += jnp.dot(m_ref[...].astype(BF16), w_ref[D_ATTN:D_ATTN + D_MLSTM, :], preferred_element_type=F32)
    acc += jnp.dot(c_ref[...].astype(BF16), w_ref[D_ATTN + D_MLSTM:, :], preferred_element_type=F32)
    o_ref[...] = x_ref[...] + acc


def _outproj(x, att, mo, cv, w_out):
    T = x.shape[0]
    tm = _tile(T, 640, LANES)
    row = lambda i: (i, 0)
    return pl.pallas_call(
        _outproj_kernel,
        grid=(T // tm,),
        in_specs=[pl.BlockSpec((tm, D_MODEL), row), pl.BlockSpec((tm, D_ATTN), row),
                  pl.BlockSpec((tm, D_MLSTM), row), pl.BlockSpec((tm, D_CONV), row),
                  pl.BlockSpec((D_MODEL, D_MODEL), lambda i: (0, 0))],
        out_specs=pl.BlockSpec((tm, D_MODEL), row),
        out_shape=jax.ShapeDtypeStruct((T, D_MODEL), F32),
        compiler_params=_params(("parallel",)),
        name="outproj",
    )(x, att, mo, cv, w_out)


def _ffn_kernel(x_ref, g_ref, w1_ref, w3_ref, w2_ref, o_ref, h_s, acc_s):
    j = pl.program_id(1)

    @pl.when(j == 0)
    def _():
        x = x_ref[...]
        h_s[...] = (x * lax.rsqrt(jnp.mean(x * x, axis=-1, keepdims=True) + EPS) * g_ref[...]).astype(BF16)
        acc_s[...] = x

    h = h_s[...]
    a = jnp.dot(h, w1_ref[...], preferred_element_type=F32)
    g = jnp.dot(h, w3_ref[...], preferred_element_type=F32)
    act = (a * jax.nn.sigmoid(a) * g).astype(BF16)
    acc_s[...] += jnp.dot(act, w2_ref[...], preferred_element_type=F32)

    @pl.when(j == pl.num_programs(1) - 1)
    def _():
        o_ref[...] = acc_s[...]


def _ffn(x, g, w1, w3, w2):
    T = x.shape[0]
    tm = _tile(T, 640, LANES)
    tf = D_FF // 2
    row = lambda i, j: (i, 0)
    return pl.pallas_call(
        _ffn_kernel,
        grid=(T // tm, D_FF // tf),
        in_specs=[pl.BlockSpec((tm, D_MODEL), row), pl.BlockSpec((1, D_MODEL), lambda i, j: (0, 0)),
                  pl.BlockSpec((D_MODEL, tf), lambda i, j: (0, j)),
                  pl.BlockSpec((D_MODEL, tf), lambda i, j: (0, j)),
                  pl.BlockSpec((tf, D_MODEL), lambda i, j: (j, 0))],
        out_specs=pl.BlockSpec((tm, D_MODEL), row),
        out_shape=jax.ShapeDtypeStruct((T, D_MODEL), F32),
        scratch_shapes=[pltpu.VMEM((tm, D_MODEL), BF16), pltpu.VMEM((tm, D_MODEL), F32)],
        compiler_params=_params(("parallel", "arbitrary")),
        name="ffn",
    )(x, g, w1, w3, w2)


def _router_kernel(x_ref, g_ref, r_ref, comb_ref):
    x = x_ref[...]
    h = x * lax.rsqrt(jnp.mean(x * x, axis=-1, keepdims=True) + EPS) * g_ref[...]
    logits = jnp.dot(h, r_ref[...], preferred_element_type=F32, precision=lax.Precision.HIGHEST)
    lane = lax.broadcasted_iota(jnp.int32, logits.shape, 1)
    logits = jnp.where(lane < N_EXPERTS, logits, NEG)
    v1 = jnp.max(logits, axis=1, keepdims=True)
    i1 = jnp.min(jnp.where(logits == v1, lane, LANES), axis=1, keepdims=True)
    rest = jnp.where(lane == i1, NEG, logits)
    v2 = jnp.max(rest, axis=1, keepdims=True)
    i2 = jnp.min(jnp.where(rest == v2, lane, LANES), axis=1, keepdims=True)
    e2 = jnp.exp(v2 - v1)
    inv = 1.0 / (1.0 + e2)
    route = jnp.where(lane == ROUTE_I1, i1.astype(F32), jnp.where(lane == ROUTE_I2, i2.astype(F32), 0.0))
    comb_ref[...] = jnp.where(lane == ROUTE_G1, inv, jnp.where(lane == ROUTE_G2, e2 * inv, route))


def _router(x, g, r_pad):
    T = x.shape[0]
    tm = _tile(T, 640, LANES)
    row = lambda i: (i, 0)
    return pl.pallas_call(
        _router_kernel,
        grid=(T // tm,),
        in_specs=[pl.BlockSpec((tm, D_MODEL), row), pl.BlockSpec((1, D_MODEL), lambda i: (0, 0)),
                  pl.BlockSpec((D_MODEL, LANES), lambda i: (0, 0))],
        out_specs=pl.BlockSpec((tm, LANES), row),
        out_shape=jax.ShapeDtypeStruct((T, LANES), F32),
        compiler_params=_params(("parallel",)),
        name="router",
    )(x, g, r_pad)


def _route_plan(i1, i2, tg, n_tiles):
    n_slots = 2 * i1.shape[0]
    e = jnp.stack([i1, i2], axis=1).reshape(-1)
    oh = (e[:, None] == jnp.arange(N_EXPERTS, dtype=jnp.int32)[None, :]).astype(jnp.int32)
    csum = jnp.cumsum(oh, axis=0)
    rank = jnp.sum(oh * csum, axis=1) - 1
    counts = csum[-1]
    tiles_per = (counts + tg - 1) // tg
    tile_end = jnp.cumsum(tiles_per)
    tile_start = tile_end - tiles_per
    pos = (tile_start[e] * tg + rank).astype(jnp.int32)
    tile_ids = jnp.arange(n_tiles, dtype=jnp.int32)
    tile_expert = jnp.minimum(jnp.sum((tile_ids[:, None] >= tile_end[None, :]).astype(jnp.int32), axis=1),
                              N_EXPERTS - 1).astype(jnp.int32)
    tile_valid = (tile_ids < tile_end[-1]).astype(jnp.int32)
    order = jnp.argsort(e, stable=True).astype(jnp.int32)
    group_off = jnp.cumsum(counts) - counts
    p = jnp.arange(n_tiles * tg, dtype=jnp.int32)
    ep = tile_expert[p // tg]
    r = p - tile_start[ep] * tg
    live = (r < counts[ep]) & (tile_valid[p // tg] > 0)
    j = jnp.clip(group_off[ep] + r, 0, n_slots - 1)
    tok_of_row = jnp.where(live, order[j] // 2, 0).astype(jnp.int32)
    return pos, tok_of_row, tile_expert, tile_valid


def _moe_group_kernel(te_ref, tv_ref, tok_ref, x_hbm, g_ref, w1_ref, w3_ref, w2_ref, o_ref, xg, sem,
                      *, tg, n_tiles):
    del te_ref
    i = pl.program_id(0)

    def gather(tile, slot):
        base = tile * tg

        def issue(r, carry):
            t = tok_ref[base + r]
            pltpu.make_async_copy(x_hbm.at[pl.ds(t, 1), :], xg.at[slot, pl.ds(r, 1), :], sem.at[slot]).start()
            return carry

        lax.fori_loop(0, tg, issue, 0, unroll=8)

    @pl.when(i == 0)
    def _():
        gather(0, 0)

    @pl.when(i + 1 < n_tiles)
    def _():
        gather(i + 1, (i + 1) % 2)

    slot = i % 2
    pltpu.make_async_copy(xg.at[slot], xg.at[slot], sem.at[slot]).wait()

    @pl.when(tv_ref[i] > 0)
    def _():
        x = xg[slot]
        h = (x * lax.rsqrt(jnp.mean(x * x, axis=-1, keepdims=True) + EPS) * g_ref[...]).astype(BF16)
        acc = jnp.zeros((tg, D_MODEL), F32)
        for c in range(0, D_FF, MOE_FF_CHUNK):
            a = jnp.dot(h, w1_ref[:, c:c + MOE_FF_CHUNK], preferred_element_type=F32)
            g = jnp.dot(h, w3_ref[:, c:c + MOE_FF_CHUNK], preferred_element_type=F32)
            act = (a * jax.nn.sigmoid(a) * g).astype(BF16)
            acc = acc + jnp.dot(act, w2_ref[c:c + MOE_FF_CHUNK, :], preferred_element_type=F32)
        o_ref[...] = acc

    @pl.when(tv_ref[i] == 0)
    def _():
        o_ref[...] = jnp.zeros((tg, D_MODEL), F32)


def _moe_group(x, g, w1, w3, w2, tok_of_row, tile_expert, tile_valid, tg, n_tiles):
    wspec = lambda shape: pl.BlockSpec((None,) + shape, lambda i, te, tv, tok: (te[i], 0, 0))
    return pl.pallas_call(
        functools.partial(_moe_group_kernel, tg=tg, n_tiles=n_tiles),
        grid_spec=pltpu.PrefetchScalarGridSpec(
            num_scalar_prefetch=3,
            grid=(n_tiles,),
            in_specs=[pl.BlockSpec(memory_space=pl.ANY),
                      pl.BlockSpec((1, D_MODEL), lambda i, te, tv, tok: (0, 0)),
                      wspec((D_MODEL, D_FF)), wspec((D_MODEL, D_FF)), wspec((D_FF, D_MODEL))],
            out_specs=pl.BlockSpec((tg, D_MODEL), lambda i, te, tv, tok: (i, 0)),
            scratch_shapes=[pltpu.VMEM((2, tg, D_MODEL), F32), pltpu.SemaphoreType.DMA((2,))]),
        out_shape=jax.ShapeDtypeStruct((n_tiles * tg, D_MODEL), F32),
        compiler_params=_params(("arbitrary",)),
        name="moe_group",
    )(tile_expert, tile_valid, tok_of_row, x, g, w1, w3, w2)


def _moe_combine_kernel(pos_ref, x_ref, route_ref, ys_hbm, o_ref, yb, sem, *, tm, n_steps):
    i = pl.program_id(0)

    def gather(step, slot):
        base = step * (2 * tm)

        def issue(r, carry):
            for choice in range(2):
                p = pos_ref[base + 2 * r + choice]
                pltpu.make_async_copy(ys_hbm.at[pl.ds(p, 1), :], yb.at[slot, choice, pl.ds(r, 1), :],
                                      sem.at[slot]).start()
            return carry

        lax.fori_loop(0, tm, issue, 0, unroll=4)

    @pl.when(i == 0)
    def _():
        gather(0, 0)

    @pl.when(i + 1 < n_steps)
    def _():
        gather(i + 1, (i + 1) % 2)

    slot = i % 2
    pltpu.make_async_copy(yb.at[slot], yb.at[slot], sem.at[slot]).wait()
    route = route_ref[...]
    g1 = route[:, ROUTE_G1:ROUTE_G1 + 1]
    g2 = route[:, ROUTE_G2:ROUTE_G2 + 1]
    o_ref[...] = x_ref[...] + g1 * yb[slot, 0] + g2 * yb[slot, 1]


def _moe_combine(x, route, ys, pos):
    T = x.shape[0]
    tm = _tile(T, 640, LANES)
    n_steps = T // tm
    row = lambda i, pos: (i, 0)
    return pl.pallas_call(
        functools.partial(_moe_combine_kernel, tm=tm, n_steps=n_steps),
        grid_spec=pltpu.PrefetchScalarGridSpec(
            num_scalar_prefetch=1,
            grid=(n_steps,),
            in_specs=[pl.BlockSpec((tm, D_MODEL), row), pl.BlockSpec((tm, LANES), row),
                      pl.BlockSpec(memory_space=pl.ANY)],
            out_specs=pl.BlockSpec((tm, D_MODEL), row),
            scratch_shapes=[pltpu.VMEM((2, 2, tm, D_MODEL), F32), pltpu.SemaphoreType.DMA((2,))]),
        out_shape=jax.ShapeDtypeStruct((T, D_MODEL), F32),
        compiler_params=_params(("arbitrary",)),
        name="moe_combine",
    )(pos, x, route, ys)


def _moe(x, g, r_pad, w1, w3, w2):
    T = x.shape[0]
    route = _router(x, g, r_pad)
    i1 = route[:, ROUTE_I1].astype(jnp.int32)
    i2 = route[:, ROUTE_I2].astype(jnp.int32)
    tg = MOE_TILE
    n_tiles = pl.cdiv(2 * T, tg) + N_EXPERTS
    pos, tok_of_row, tile_expert, tile_valid = _route_plan(i1, i2, tg, n_tiles)
    ys = _moe_group(x, g, w1, w3, w2, tok_of_row, tile_expert, tile_valid, tg, n_tiles)
    return _moe_combine(x, route, ys, pos)


def _rope_tables(pos):
    half = ROPE_DIM // 2
    inv_freq = ROPE_THETA ** (-2.0 * jnp.arange(half, dtype=F32) / ROPE_DIM)
    ang = pos[:, None] * inv_freq[None, :]
    cos, sin = jnp.cos(ang), jnp.sin(ang)
    n = pos.shape[0]
    ones = jnp.ones((n, HEAD_DIM - ROPE_DIM), F32)
    zeros = jnp.zeros((n, HEAD_DIM - ROPE_DIM), F32)
    zh = jnp.zeros((n, half), F32)
    c = jnp.concatenate([cos, cos, ones], axis=1)
    s1 = jnp.concatenate([zh, sin, zeros], axis=1)
    s2 = jnp.concatenate([-sin, zh, zeros], axis=1)
    two = lambda t: jnp.concatenate([t, t], axis=1)
    return two(c), two(s1), two(s2)


def _pair_state(c, n, m):
    nb = c.shape[0]
    c = c.reshape(nb, 2, 2, HEAD_DIM, HEAD_DIM)
    z = jnp.zeros((nb, 2, HEAD_DIM, HEAD_DIM), F32)
    top = jnp.concatenate([c[:, :, 0], z], axis=-1)
    bot = jnp.concatenate([z, c[:, :, 1]], axis=-1)
    cp = jnp.concatenate([top, bot], axis=-2)
    nm = jnp.zeros((nb, 8, LANES), F32)
    nm = nm.at[:, 0:2, :].set(n.reshape(nb, 2, LANES))
    nm = nm.at[:, 2, 0:N_HEADS_MLSTM].set(m)
    return cp, nm


def _unpair_state(cp, nm):
    nb = cp.shape[0]
    c = jnp.stack([cp[:, :, :HEAD_DIM, :HEAD_DIM], cp[:, :, HEAD_DIM:, HEAD_DIM:]], axis=2)
    c = c.reshape(nb, N_HEADS_MLSTM, HEAD_DIM, HEAD_DIM)
    n = nm[:, 0:2, :].reshape(nb, N_HEADS_MLSTM, HEAD_DIM)
    m = nm[:, 2, 0:N_HEADS_MLSTM]
    return c, n, m


def kernel(x_prompt, x_sample, cache_attn_k, cache_attn_v, state_mlstm_C, state_mlstm_n, state_mlstm_m, state_conv, norm_mix, w_in, q_norm, k_norm, mlstm_gate_bias, mlstm_out_norm, conv_dw_w, conv_dw_b, conv_norm_g, conv_norm_b, w_out, norm_ffn, ffn_w1, ffn_w3, ffn_w2, moe_router, moe_w1, moe_w3, moe_w2):
    batch, seq, _ = x_prompt.shape
    nbs, dec, _ = x_sample.shape
    depth = w_in.shape[0]
    tp = batch * seq
    keep = min(max(w for w, _ in DILATED_BRANCHES), seq)
    hist_rows = CONV_WIDTH - 1

    x = jnp.concatenate([x_prompt.reshape(tp, D_MODEL), x_sample.reshape(nbs * dec, D_MODEL)], axis=0)

    pos = jnp.concatenate([jnp.tile(jnp.arange(seq, dtype=F32), batch),
                           jnp.tile(PAST_LEN + jnp.arange(dec, dtype=F32), nbs)])
    rc, rs1, rs2 = _rope_tables(pos)
    gi = np.arange(D_ATTN) // HEAD_DIM
    bd = jnp.asarray((gi[:, None] == gi[None, :]).astype(np.float32) / HEAD_DIM, dtype=BF16)

    p_len = cache_attn_k.shape[2]
    cache_k = cache_attn_k.reshape(depth, nbs, p_len, D_ATTN)
    cache_v = cache_attn_v.reshape(depth, nbs, p_len, D_ATTN)

    zero_c = jnp.zeros((batch, 2, LANES, LANES), F32)
    zero_nm = jnp.zeros((batch, 8, LANES), F32)
    zero_hist = jnp.zeros((batch, HIST_PAD, D_CONV), F32)

    outs = [[] for _ in range(12)]
    for l in range(depth):
        o = 3 * D_ATTN + 4 * D_MLSTM
        w_p = jnp.concatenate([w_in[l][:, :o], jnp.pad(w_in[l][:, o:o + 8], ((0, 0), (0, GATE_PAD - 8))),
                               w_in[l][:, o + 8:]], axis=1).astype(BF16)
        qn_row = jnp.tile(q_norm[l], N_HEADS_ATTN)[None, :]
        kn_row = jnp.tile(k_norm[l], N_HEADS_ATTN)[None, :]
        gb_row = jnp.pad(mlstm_gate_bias[l], (0, GATE_PAD - 2 * N_HEADS_MLSTM))[None, :]

        q, k, v, zm, gt, u = _inproj(x, norm_mix[l][None, :], w_p, qn_row, kn_row, gb_row, bd, rc, rs1, rs2)

        att_p = _attn_prompt(q, k, v, batch, seq)
        on_row = mlstm_out_norm[l][None, :]
        hm_p, c1_p, nm1_p = _mlstm(zm, gt, zero_c, zero_nm, on_row, batch, seq)
        w_conv = jnp.pad(conv_dw_w[l], ((0, HIST_PAD - CONV_WIDTH), (0, 0)))
        conv_args = (w_conv, conv_dw_b[l][None, :], conv_norm_g[l][None, :], conv_norm_b[l][None, :], bd)
        cv_p, st_p = _conv(u, zero_hist, *conv_args, batch, seq)

        sq = lambda t: t[tp:].reshape(nbs, dec, t.shape[1])
        att_s = _attn_sample(sq(q), sq(k), sq(v), cache_k, cache_v, l)
        c0_s, nm0_s = _pair_state(state_mlstm_C[l], state_mlstm_n[l], state_mlstm_m[l])
        hm_s, c1_s, nm1_s = _mlstm(zm[tp:], gt[tp:], c0_s, nm0_s, on_row, nbs, dec)
        hist_s = jnp.pad(state_conv[l], ((0, 0), (HIST_PAD - hist_rows, 0), (0, 0)))
        cv_s, st_s = _conv(u[tp:], hist_s, *conv_args, nbs, dec)

        att = jnp.concatenate([att_p, att_s.reshape(nbs * dec, D_ATTN)], axis=0)
        hm = jnp.concatenate([hm_p, hm_s], axis=0)
        cv = jnp.concatenate([cv_p, cv_s], axis=0)
        x = _outproj(x, att, hm, cv, w_out[l].astype(BF16))

        i = l // 2
        if l % 2 == 0:
            x = _ffn(x, norm_ffn[l][None, :], ffn_w1[i].astype(BF16), ffn_w3[i].astype(BF16),
                     ffn_w2[i].astype(BF16))
        else:
            r_pad = jnp.pad(moe_router[i], ((0, 0), (0, LANES - N_EXPERTS)))
            x = _moe(x, norm_ffn[l][None, :], r_pad, moe_w1[i].astype(BF16), moe_w3[i].astype(BF16),
                     moe_w2[i].astype(BF16))

        kp = k[:tp].reshape(batch, seq, N_HEADS_ATTN, HEAD_DIM)[:, seq - keep:]
        vp = v[:tp].reshape(batch, seq, N_HEADS_ATTN, HEAD_DIM)[:, seq - keep:]
        cp_, np_, mp_ = _unpair_state(c1_p, nm1_p)
        cs_, ns_, ms_ = _unpair_state(c1_s, nm1_s)
        layer_outs = (kp, vp, sq(k).reshape(nbs, dec, N_HEADS_ATTN, HEAD_DIM),
                      sq(v).reshape(nbs, dec, N_HEADS_ATTN, HEAD_DIM),
                      cp_, np_, mp_, cs_, ns_, ms_,
                      st_p[:, HIST_PAD - hist_rows:], st_s[:, HIST_PAD - hist_rows:])
        for lst, val in zip(outs, layer_outs):
            lst.append(val)

    y_prompt = x[:tp].reshape(batch, seq, D_MODEL)
    y_sample = x[tp:].reshape(nbs, dec, D_MODEL)
    return (y_prompt, y_sample) + tuple(jnp.stack(lst, axis=0) for lst in outs)
```

```python
import functools

import numpy as np
import jax
import jax.numpy as jnp
from jax import lax
from jax.experimental import pallas as pl
from jax.experimental.pallas import tpu as pltpu

F32 = jnp.float32
BF16 = jnp.bfloat16

D_MODEL = 1024
HEAD_DIM = 64
N_HEADS_ATTN = 6
N_HEADS_MLSTM = 4
D_ATTN = N_HEADS_ATTN * HEAD_DIM
D_MLSTM = N_HEADS_MLSTM * HEAD_DIM
D_CONV = 384
DILATED_BRANCHES = ((128, 1), (512, 4), (2048, 16))
ATTN_BLOCK = 128
ROPE_DIM = HEAD_DIM // 4
ROPE_THETA = 500000.0
MLSTM_CHUNK = 128
CONV_WIDTH = 31
D_FF = 2816
N_EXPERTS = 8
EPS = 1e-6
NEG = -1e30
PAST_LEN = 16384

LANES = 128
SUBLANES = 8
ROW_CHUNKS = D_MODEL // LANES
HIST_PAD = 32
GATE_PAD = LANES
W_IN_COLS = 3 * D_ATTN + 4 * D_MLSTM + GATE_PAD + 2 * D_CONV
VMEM_LIMIT = 56 * 1024 * 1024
MOE_TILE = 512
MOE_FF_CHUNK = 256
ROUTE_I1, ROUTE_I2, ROUTE_G1, ROUTE_G2 = 8, 9, 10, 11


def _tile(n, pref, mult=8):
    for t in range(min(pref, n), 0, -1):
        if n % t == 0 and t % mult == 0:
            return t
    return n


def _params(sem):
    return pltpu.CompilerParams(dimension_semantics=sem, vmem_limit_bytes=VMEM_LIMIT)


def _split_dot(a, b_bf16):
    hi = a.astype(BF16)
    lo = (a - hi.astype(F32)).astype(BF16)
    return (jnp.dot(hi, b_bf16, preferred_element_type=F32)
            + jnp.dot(lo, b_bf16, preferred_element_type=F32))


def _group_mean(a, bd_ref):
    return _split_dot(a, bd_ref[...])


def _inproj_kernel(x_ref, g_ref, w_ref, qn_ref, kn_ref, gb_ref, bd_ref, rc_ref, rs1_ref, rs2_ref,
                   q_ref, k_ref, v_ref, zm_ref, gt_ref, u_ref):
    x = x_ref[...]
    h = (x * lax.rsqrt(jnp.mean(x * x, axis=-1, keepdims=True) + EPS) * g_ref[...]).astype(BF16)

    def proj(lo, hi):
        return jnp.dot(h, w_ref[:, lo:hi], preferred_element_type=F32)

    rc, rs1, rs2 = rc_ref[...], rs1_ref[...], rs2_ref[...]

    def norm_rope(z, gain_ref, scale):
        zn = z * lax.rsqrt(_group_mean(z * z, bd_ref) + EPS) * gain_ref[...]
        outs = []
        for c in range(D_ATTN // LANES):
            zc = zn[:, c * LANES:(c + 1) * LANES]
            y = (zc * rc + pltpu.roll(zc, ROPE_DIM // 2, 1) * rs1
                 + pltpu.roll(zc, LANES - ROPE_DIM // 2, 1) * rs2)
            outs.append(y * scale if scale != 1.0 else y)
        return jnp.concatenate(outs, axis=1)

    q_ref[...] = norm_rope(proj(0, D_ATTN), qn_ref, HEAD_DIM ** -0.5)
    k_ref[...] = norm_rope(proj(D_ATTN, 2 * D_ATTN), kn_ref, 1.0)
    v_ref[...] = proj(2 * D_ATTN, 3 * D_ATTN)

    o = 3 * D_ATTN
    zm = proj(o, o + 4 * D_MLSTM)
    lane = lax.broadcasted_iota(jnp.int32, (1, 4 * D_MLSTM), 1)
    zm_ref[...] = zm * jnp.where((lane >= D_MLSTM) & (lane < 2 * D_MLSTM), HEAD_DIM ** -0.5, 1.0)

    o += 4 * D_MLSTM
    gt = proj(o, o + GATE_PAD) + gb_ref[...]
    glane = lax.broadcasted_iota(jnp.int32, (1, GATE_PAD), 1)
    log_sig = jnp.minimum(gt, 0.0) - jnp.log1p(jnp.exp(-jnp.abs(gt)))
    gt_ref[...] = jnp.where(glane < N_HEADS_MLSTM, gt, log_sig)

    o += GATE_PAD
    cv = proj(o, o + D_CONV)
    cg = proj(o + D_CONV, o + 2 * D_CONV)
    u_ref[...] = cv * jax.nn.sigmoid(cg)


def _inproj(x, g, w_p, qn_row, kn_row, gb_row, bd, rc, rs1, rs2):
    T = x.shape[0]
    tm = _tile(T, 640, LANES)
    row = lambda i: (i, 0)
    const = lambda i: (0, 0)
    out_shapes = [jax.ShapeDtypeStruct((T, n), F32)
                  for n in (D_ATTN, D_ATTN, D_ATTN, 4 * D_MLSTM, GATE_PAD, D_CONV)]
    return pl.pallas_call(
        _inproj_kernel,
        grid=(T // tm,),
        in_specs=[pl.BlockSpec((tm, D_MODEL), row),
                  pl.BlockSpec((1, D_MODEL), const),
                  pl.BlockSpec((D_MODEL, W_IN_COLS), const),
                  pl.BlockSpec((1, D_ATTN), const),
                  pl.BlockSpec((1, D_ATTN), const),
                  pl.BlockSpec((1, GATE_PAD), const),
                  pl.BlockSpec((D_ATTN, D_ATTN), const),
                  pl.BlockSpec((tm, LANES), row),
                  pl.BlockSpec((tm, LANES), row),
                  pl.BlockSpec((tm, LANES), row)],
        out_specs=[pl.BlockSpec((tm, s.shape[1]), row) for s in out_shapes],
        out_shape=out_shapes,
        compiler_params=_params(("parallel",)),
        name="inproj",
    )(x, g, w_p, qn_row, kn_row, gb_row, bd, rc, rs1, rs2)


def _attn_prompt_kernel(q_ref, k_ref, v_ref, o_ref, qs, ks, vs, racc, rm, rden, nacc, nm, nden, *, seq):
    blk = ATTN_BLOCK
    n_blocks = seq // blk
    lane = lax.broadcasted_iota(jnp.int32, (1, LANES), 1)
    head_a = lane < HEAD_DIM
    row = lax.broadcasted_iota(jnp.int32, (blk, 2 * blk), 0)
    col = lax.broadcasted_iota(jnp.int32, (blk, 2 * blk), 1)
    band = (col >= row) & (col <= row + blk)
    bias_band = jnp.where(band, 0.0, NEG)
    bias_first = jnp.where(band & (col >= blk), 0.0, NEG)

    ks[0:blk, :] = jnp.zeros((blk, LANES), BF16)
    vs[0:blk, :] = jnp.zeros((blk, LANES), BF16)

    cp = 256
    for (_, dil) in DILATED_BRANCHES:
        nsub = seq // dil
        nb = nsub // blk
        for r in range(dil):
            for c0 in range(0, nsub, cp):
                n = min(cp, nsub - c0)
                if dil == 1:
                    src = pl.ds(c0, n)
                else:
                    src = pl.ds(r + c0 * dil, n, stride=dil)
                dst = r * nsub + c0
                qs[dst:dst + n, :] = q_ref[src, :].astype(BF16)
                ks[blk + dst:blk + dst + n, :] = k_ref[src, :].astype(BF16)
                vs[blk + dst:blk + dst + n, :] = v_ref[src, :].astype(BF16)

        if dil == 1:
            dst_acc, dst_m, dst_den = nacc, nm, nden
        else:
            dst_acc, dst_m, dst_den = racc, rm, rden

        def body(j, carry, nb=nb, dst_acc=dst_acc, dst_m=dst_m, dst_den=dst_den):
            off = pl.multiple_of(j * blk, blk)
            qb = qs[pl.ds(off, blk), :]
            zero = jnp.zeros_like(qb)
            q2 = jnp.concatenate([jnp.where(head_a, qb, zero), jnp.where(head_a, zero, qb)], axis=0)
            kk = ks[pl.ds(off, 2 * blk), :]
            vv = vs[pl.ds(off, 2 * blk), :]
            s = lax.dot_general(q2, kk, (((1,), (1,)), ((), ())), preferred_element_type=F32)
            bias = jnp.where(j % nb == 0, bias_first, bias_band)
            s = s + jnp.concatenate([bias, bias], axis=0)
            mx = jnp.max(s, axis=1, keepdims=True)
            p = jnp.exp(s - mx)
            den = jnp.sum(p, axis=1, keepdims=True)
            o = jnp.dot(p.astype(BF16), vv, preferred_element_type=F32)
            dst_acc[pl.ds(off, blk), :] = jnp.where(head_a, o[:blk], o[blk:])
            dst_m[pl.ds(off, blk), :] = jnp.where(head_a, mx[:blk], mx[blk:])
            dst_den[pl.ds(off, blk), :] = jnp.where(head_a, den[:blk], den[blk:])
            return carry

        lax.fori_loop(0, n_blocks, body, 0, unroll=4)

        if dil > 1:
            for r in range(dil):
                for c0 in range(0, nsub, cp):
                    n = min(cp, nsub - c0)
                    nat = pl.ds(r + c0 * dil, n, stride=dil)
                    res = pl.ds(r * nsub + c0, n)
                    m_old, m_new = nm[nat, :], rm[res, :]
                    m_all = jnp.maximum(m_old, m_new)
                    e_old = jnp.exp(m_old - m_all)
                    e_new = jnp.exp(m_new - m_all)
                    nacc[nat, :] = e_old * nacc[nat, :] + e_new * racc[res, :]
                    nden[nat, :] = e_old * nden[nat, :] + e_new * rden[res, :]
                    nm[nat, :] = m_all

    o_ref[...] = nacc[...] / nden[...]


def _attn_prompt(q, k, v, batch, seq):
    n_pairs = D_ATTN // LANES
    spec = pl.BlockSpec((seq, LANES), lambda b, hp: (b, hp))
    return pl.pallas_call(
        functools.partial(_attn_prompt_kernel, seq=seq),
        grid=(batch, n_pairs),
        in_specs=[spec, spec, spec],
        out_specs=spec,
        out_shape=jax.ShapeDtypeStruct((batch * seq, D_ATTN), F32),
        scratch_shapes=[pltpu.VMEM((seq, LANES), BF16),
                        pltpu.VMEM((seq + ATTN_BLOCK, LANES), BF16),
                        pltpu.VMEM((seq + ATTN_BLOCK, LANES), BF16)]
                       + [pltpu.VMEM((seq, LANES), F32) for _ in range(6)],
        compiler_params=_params(("parallel", "parallel")),
        name="attn_prompt",
    )(q, k, v)


def _attn_sample_kernel(q_ref, kn_ref, vn_ref, kc_ref, vc_ref, cw_ref, o_ref, kall, vall, *, p_len, dec):
    pad_rows = kall.shape[0] - p_len
    kall[0:p_len, :] = kc_ref[...].astype(BF16)
    vall[0:p_len, :] = vc_ref[...].astype(BF16)
    tail = jnp.zeros((pad_rows - dec, D_ATTN), F32)
    kall[p_len:, :] = jnp.concatenate([kn_ref[...], tail], axis=0).astype(BF16)
    vall[p_len:, :] = jnp.concatenate([vn_ref[...], tail], axis=0).astype(BF16)

    lane = lax.broadcasted_iota(jnp.int32, (1, LANES), 1)
    head_a = lane < HEAD_DIM
    cw = cw_ref[...]
    cw2 = jnp.concatenate([cw, cw], axis=0)
    outs = []
    for hp in range(D_ATTN // LANES):
        sl = slice(hp * LANES, (hp + 1) * LANES)
        qb = q_ref[:, sl].astype(BF16)
        zero = jnp.zeros_like(qb)
        q2 = jnp.concatenate([jnp.where(head_a, qb, zero), jnp.where(head_a, zero, qb)], axis=0)
        s = lax.dot_general(q2, kall[:, sl], (((1,), (1,)), ((), ())), preferred_element_type=F32)
        s = jnp.where(cw2 > 0.0, s, NEG)
        mx = jnp.max(s, axis=1, keepdims=True)
        p = jnp.exp(s - mx) * cw2
        den = jnp.sum(p, axis=1, keepdims=True)
        o = jnp.dot(p.astype(BF16), vall[:, sl], preferred_element_type=F32) / den
        outs.append(jnp.where(head_a, o[:dec], o[dec:]))
    o_ref[...] = jnp.concatenate(outs, axis=1)


def _sample_multiplicity(p_len, dec, n_rows):
    t = np.arange(dec)[:, None]
    i = np.arange(n_rows)[None, :]
    dist = p_len + t - i
    cw = np.zeros((dec, n_rows), np.float32)
    for (w, d) in DILATED_BRANCHES:
        cw += ((dist >= 0) & (dist <= w) & (dist % d == 0) & (i < p_len + dec))
    return jnp.asarray(cw)


def _attn_sample(q, k, v, cache_k, cache_v, layer):
    nb, dec, _ = q.shape
    p_len = cache_k.shape[2]
    n_rows = p_len + LANES
    cw = _sample_multiplicity(p_len, dec, n_rows)
    new = pl.BlockSpec((None, dec, D_ATTN), lambda b: (b, 0, 0))
    cache = pl.BlockSpec((None, None, p_len, D_ATTN), lambda b: (layer, b, 0, 0))
    return pl.pallas_call(
        functools.partial(_attn_sample_kernel, p_len=p_len, dec=dec),
        grid=(nb,),
        in_specs=[new, new, new, cache, cache, pl.BlockSpec((dec, n_rows), lambda b: (0, 0))],
        out_specs=new,
        out_shape=jax.ShapeDtypeStruct((nb, dec, D_ATTN), F32),
        scratch_shapes=[pltpu.VMEM((n_rows, D_ATTN), BF16), pltpu.VMEM((n_rows, D_ATTN), BF16)],
        compiler_params=_params(("parallel",)),
        name="attn_sample",
    )(q, k, v, cache_k, cache_v, cw)


def _mlstm_kernel(zm_ref, gt_ref, c0_ref, nm0_ref, on_ref, h_ref, c1_ref, nm1_ref, c_s, nm_s, *, rows):
    ch = MLSTM_CHUNK
    step = pl.program_id(1)

    @pl.when(step == 0)
    def _():
        c_s[...] = c0_ref[...]
        nm_s[...] = nm0_ref[...]

    def padded(a):
        if rows == ch:
            return a
        return jnp.concatenate([a, jnp.zeros((ch - rows, a.shape[1]), a.dtype)], axis=0)

    lane = lax.broadcasted_iota(jnp.int32, (1, LANES), 1)
    head_a = lane < HEAD_DIM
    ri = lax.broadcasted_iota(jnp.int32, (ch, ch), 0)
    ci = lax.broadcasted_iota(jnp.int32, (ch, ch), 1)
    causal = ci <= ri
    eye = ci == ri
    rowv = lax.broadcasted_iota(jnp.int32, (ch, 1), 0)
    valid = rowv < rows

    gt = padded(gt_ref[...])
    glane = lax.broadcasted_iota(jnp.int32, (1, LANES), 1)
    is_ig = glane < N_HEADS_MLSTM
    gt = jnp.where(valid, gt, jnp.where(is_ig, NEG, 0.0))
    lf_only = jnp.where(is_ig, 0.0, gt)
    fcum = jnp.dot(causal.astype(F32), lf_only, preferred_element_type=F32,
                   precision=lax.Precision.HIGHEST)

    nm = nm_s[...]
    m_row_new = nm[2:3, :]
    brow128 = lax.broadcasted_iota(jnp.int32, (LANES, 1), 0) < HEAD_DIM
    bd_mask = brow128 == head_a

    for pair in range(N_HEADS_MLSTM // 2):
        ls = slice(pair * LANES, (pair + 1) * LANES)
        qp = padded(zm_ref[:, ls])
        kp = padded(zm_ref[:, 2 * LANES + pair * LANES:2 * LANES + (pair + 1) * LANES])
        vp = padded(zm_ref[:, 4 * LANES + pair * LANES:4 * LANES + (pair + 1) * LANES])
        op = padded(zm_ref[:, 6 * LANES + pair * LANES:6 * LANES + (pair + 1) * LANES])
        kb, vb = kp.astype(BF16), vp.astype(BF16)
        c_pair = c_s[pair]
        cb = c_pair.astype(BF16)
        n_row = nm[pair:pair + 1, :]

        h_heads, wl_heads, decays = [], [], []
        for sub in range(2):
            hx = 2 * pair + sub
            mask = head_a if sub == 0 else jnp.logical_not(head_a)
            a = fcum[:, N_HEADS_MLSTM + hx:N_HEADS_MLSTM + hx + 1]
            igc = gt[:, hx:hx + 1]
            m0 = nm[2:3, hx:hx + 1]
            brow = jnp.sum(jnp.where(eye, igc - a, 0.0), axis=0, keepdims=True)
            dmat = jnp.where(causal, a + brow, NEG)
            inter = a + m0
            m = jnp.maximum(inter, jnp.max(dmat, axis=1, keepdims=True))
            w = jnp.exp(dmat - m)
            g = jnp.exp(inter - m)
            qx = jnp.where(mask, qp, 0.0)
            qxb = qx.astype(BF16)
            sc = lax.dot_general(qxb, kb, (((1,), (1,)), ((), ())), preferred_element_type=F32) * w
            num = (jnp.dot(sc.astype(BF16), vb, preferred_element_type=F32)
                   + g * jnp.dot(qxb, cb, preferred_element_type=F32))
            den = (jnp.sum(sc, axis=1, keepdims=True)
                   + g * jnp.sum(qx * n_row, axis=1, keepdims=True))
            h_heads.append(num / jnp.maximum(jnp.abs(den), jnp.exp(-m)))
            m_last = m[rows - 1:rows, :]
            f_last = a[rows - 1:rows, :]
            decays.append(jnp.exp(f_last + m0 - m_last))
            wl_heads.append(jnp.exp(f_last - a + igc - m_last))
            m_row_new = jnp.where(glane == hx, m_last, m_row_new)

        kw = kp * jnp.where(head_a, wl_heads[0], wl_heads[1])
        upd = lax.dot_general(kw.astype(BF16), vb, (((0,), (0,)), ((), ())), preferred_element_type=F32)
        c_s[pair] = jnp.where(bd_mask, jnp.where(brow128, decays[0], decays[1]) * c_pair + upd, 0.0)
        n_new = jnp.where(head_a, decays[0], decays[1]) * n_row + jnp.sum(kw, axis=0, keepdims=True)
        nm_s[pair:pair + 1, :] = n_new

        y = jax.nn.sigmoid(op) * jnp.where(head_a, h_heads[0], h_heads[1])
        y2 = y * y
        ss_a = jnp.sum(jnp.where(head_a, y2, 0.0), axis=1, keepdims=True)
        ss_b = jnp.sum(jnp.where(head_a, 0.0, y2), axis=1, keepdims=True)
        ms = jnp.where(head_a, ss_a, ss_b) * (1.0 / HEAD_DIM)
        out = y * lax.rsqrt(ms + EPS) * on_ref[:, ls]
        h_ref[:, ls] = out[:rows]

    nm_s[2:3, :] = m_row_new

    @pl.when(step == pl.num_programs(1) - 1)
    def _():
        c1_ref[...] = c_s[...]
        nm1_ref[...] = nm_s[...]


def _mlstm(zm, gt, c0, nm0, on_row, nb, seq):
    rows = min(seq, MLSTM_CHUNK)
    n_chunks = seq // rows
    tok = lambda b, c: (b * n_chunks + c, 0)
    state = lambda b, c: (b, 0, 0, 0)
    return pl.pallas_call(
        functools.partial(_mlstm_kernel, rows=rows),
        grid=(nb, n_chunks),
        in_specs=[pl.BlockSpec((rows, 4 * D_MLSTM), tok),
                  pl.BlockSpec((rows, GATE_PAD), tok),
                  pl.BlockSpec((None, 2, LANES, LANES), state),
                  pl.BlockSpec((None, 8, LANES), lambda b, c: (b, 0, 0)),
                  pl.BlockSpec((1, D_MLSTM), lambda b, c: (0, 0))],
        out_specs=[pl.BlockSpec((rows, D_MLSTM), tok),
                   pl.BlockSpec((None, 2, LANES, LANES), state),
                   pl.BlockSpec((None, 8, LANES), lambda b, c: (b, 0, 0))],
        out_shape=[jax.ShapeDtypeStruct((nb * seq, D_MLSTM), F32),
                   jax.ShapeDtypeStruct((nb, 2, LANES, LANES), F32),
                   jax.ShapeDtypeStruct((nb, 8, LANES), F32)],
        scratch_shapes=[pltpu.VMEM((2, LANES, LANES), F32), pltpu.VMEM((8, LANES), F32)],
        compiler_params=_params(("parallel", "arbitrary")),
        name="mlstm",
    )(zm, gt, c0, nm0, on_row)


def _conv_kernel(u_ref, hist_ref, w_ref, b_ref, g_ref, nb_ref, bd_ref, c_ref, st_ref, uc, *, seq, rt):
    lead = HIST_PAD - (CONV_WIDTH - 1)
    uc[0:HIST_PAD, :] = hist_ref[...]
    uc[HIST_PAD:HIST_PAD + seq, :] = u_ref[...]
    uc[HIST_PAD + seq:, :] = jnp.zeros((SUBLANES, D_CONV), F32)
    w = w_ref[...]

    def chunk(c, carry):
        base = pl.multiple_of(c * rt, rt)
        win = uc[pl.ds(base, rt + HIST_PAD + SUBLANES), :]
        acc = None
        for b in range(SUBLANES):
            inner = None
            for a in range((lead + CONV_WIDTH - 1) // SUBLANES + 1):
                j = SUBLANES * a + b - lead
                if 0 <= j < CONV_WIDTH:
                    term = win[SUBLANES * a:SUBLANES * a + rt + SUBLANES, :] * w[j:j + 1, :]
                    inner = term if inner is None else inner + term
            part = inner[b:b + rt, :]
            acc = part if acc is None else acc + part
        y = acc + b_ref[...]
        mu = _group_mean(y, bd_ref)
        yc = y - mu
        var = _group_mean(yc * yc, bd_ref)
        z = yc * lax.rsqrt(var + EPS) * g_ref[...] + nb_ref[...]
        c_ref[pl.ds(base, rt), :] = z * jax.nn.sigmoid(z)
        return carry

    lax.fori_loop(0, seq // rt, chunk, 0)
    st_ref[...] = uc[seq:seq + HIST_PAD, :]


def _conv(u, hist, w, b, g, nbias, bd, nb, seq):
    rt = min(seq, 256)
    tok = lambda i: (i, 0)
    const = lambda i: (0, 0)
    hspec = pl.BlockSpec((None, HIST_PAD, D_CONV), lambda i: (i, 0, 0))
    return pl.pallas_call(
        functools.partial(_conv_kernel, seq=seq, rt=rt),
        grid=(nb,),
        in_specs=[pl.BlockSpec((seq, D_CONV), tok), hspec,
                  pl.BlockSpec((HIST_PAD, D_CONV), const),
                  pl.BlockSpec((1, D_CONV), const), pl.BlockSpec((1, D_CONV), const),
                  pl.BlockSpec((1, D_CONV), const), pl.BlockSpec((D_CONV, D_CONV), const)],
        out_specs=[pl.BlockSpec((seq, D_CONV), tok), hspec],
        out_shape=[jax.ShapeDtypeStruct((nb * seq, D_CONV), F32),
                   jax.ShapeDtypeStruct((nb, HIST_PAD, D_CONV), F32)],
        scratch_shapes=[pltpu.VMEM((seq + HIST_PAD + SUBLANES, D_CONV), F32)],
        compiler_params=_params(("parallel",)),
        name="conv",
    )(u, hist, w, b, g, nbias, bd)


def _outproj_kernel(x_ref, a_ref, m_ref, c_ref, w_ref, o_ref):
    acc = jnp.dot(a_ref[...].astype(BF16), w_ref[0:D_ATTN, :], preferred_element_type=F32)
    acc += jnp.dot(m_ref[...].astype(BF16), w_ref[D_ATTN:D_ATTN + D_MLSTM, :], preferred_element_type=F32)
    acc += jnp.dot(c_ref[...].astype(BF16), w_ref[D_ATTN + D_MLSTM:, :], preferred_element_type=F32)
    o_ref[...] = x_ref[...] + acc


def _outproj(x, att, mo, cv, w_out):
    T = x.shape[0]
    tm = _tile(T, 640, LANES)
    row = lambda i: (i, 0)
    return pl.pallas_call(
        _outproj_kernel,
        grid=(T // tm,),
        in_specs=[pl.BlockSpec((tm, D_MODEL), row), pl.BlockSpec((tm, D_ATTN), row),
                  pl.BlockSpec((tm, D_MLSTM), row), pl.BlockSpec((tm, D_CONV), row),
                  pl.BlockSpec((D_MODEL, D_MODEL), lambda i: (0, 0))],
        out_specs=pl.BlockSpec((tm, D_MODEL), row),
        out_shape=jax.ShapeDtypeStruct((T, D_MODEL), F32),
        compiler_params=_params(("parallel",)),
        name="outproj",
    )(x, att, mo, cv, w_out)


def _ffn_kernel(x_ref, g_ref, w1_ref, w3_ref, w2_ref, o_ref, h_s, acc_s):
    j = pl.program_id(1)

    @pl.when(j == 0)
    def _():
        x = x_ref[...]
        h_s[...] = (x * lax.rsqrt(jnp.mean(x * x, axis=-1, keepdims=True) + EPS) * g_ref[...]).astype(BF16)
        acc_s[...] = x

    h = h_s[...]
    a = jnp.dot(h, w1_ref[...], preferred_element_type=F32)
    g = jnp.dot(h, w3_ref[...], preferred_element_type=F32)
    act = (a * jax.nn.sigmoid(a) * g).astype(BF16)
    acc_s[...] += jnp.dot(act, w2_ref[...], preferred_element_type=F32)

    @pl.when(j == pl.num_programs(1) - 1)
    def _():
        o_ref[...] = acc_s[...]


def _ffn(x, g, w1, w3, w2):
    T = x.shape[0]
    tm = _tile(T, 640, LANES)
    tf = D_FF // 2
    row = lambda i, j: (i, 0)
    return pl.pallas_call(
        _ffn_kernel,
        grid=(T // tm, D_FF // tf),
        in_specs=[pl.BlockSpec((tm, D_MODEL), row), pl.BlockSpec((1, D_MODEL), lambda i, j: (0, 0)),
                  pl.BlockSpec((D_MODEL, tf), lambda i, j: (0, j)),
                  pl.BlockSpec((D_MODEL, tf), lambda i, j: (0, j)),
                  pl.BlockSpec((tf, D_MODEL), lambda i, j: (j, 0))],
        out_specs=pl.BlockSpec((tm, D_MODEL), row),
        out_shape=jax.ShapeDtypeStruct((T, D_MODEL), F32),
        scratch_shapes=[pltpu.VMEM((tm, D_MODEL), BF16), pltpu.VMEM((tm, D_MODEL), F32)],
        compiler_params=_params(("parallel", "arbitrary")),
        name="ffn",
    )(x, g, w1, w3, w2)


def _to_row_tiles(ref, val):
    n = val.shape[0]
    for c in range(ROW_CHUNKS):
        ref[pl.ds(c, n, stride=ROW_CHUNKS), :] = val[:, c * LANES:(c + 1) * LANES]


def _from_row_tiles(ref, n):
    return jnp.concatenate([ref[pl.ds(c, n, stride=ROW_CHUNKS), :] for c in range(ROW_CHUNKS)], axis=1)


def _router_kernel(x_ref, g_ref, r_ref, comb_ref, xt_ref):
    x = x_ref[...]
    _to_row_tiles(xt_ref, x)
    h = x * lax.rsqrt(jnp.mean(x * x, axis=-1, keepdims=True) + EPS) * g_ref[...]
    logits = jnp.dot(h, r_ref[...], preferred_element_type=F32, precision=lax.Precision.HIGHEST)
    lane = lax.broadcasted_iota(jnp.int32, logits.shape, 1)
    logits = jnp.where(lane < N_EXPERTS, logits, NEG)
    v1 = jnp.max(logits, axis=1, keepdims=True)
    i1 = jnp.min(jnp.where(logits == v1, lane, LANES), axis=1, keepdims=True)
    rest = jnp.where(lane == i1, NEG, logits)
    v2 = jnp.max(rest, axis=1, keepdims=True)
    i2 = jnp.min(jnp.where(rest == v2, lane, LANES), axis=1, keepdims=True)
    e2 = jnp.exp(v2 - v1)
    inv = 1.0 / (1.0 + e2)
    route = jnp.where(lane == ROUTE_I1, i1.astype(F32), jnp.where(lane == ROUTE_I2, i2.astype(F32), 0.0))
    comb_ref[...] = jnp.where(lane == ROUTE_G1, inv, jnp.where(lane == ROUTE_G2, e2 * inv, route))


def _router(x, g, r_pad):
    T = x.shape[0]
    tm = _tile(T, 640, LANES)
    row = lambda i: (i, 0)
    return pl.pallas_call(
        _router_kernel,
        grid=(T // tm,),
        in_specs=[pl.BlockSpec((tm, D_MODEL), row), pl.BlockSpec((1, D_MODEL), lambda i: (0, 0)),
                  pl.BlockSpec((D_MODEL, LANES), lambda i: (0, 0))],
        out_specs=[pl.BlockSpec((tm, LANES), row), pl.BlockSpec((tm * ROW_CHUNKS, LANES), row)],
        out_shape=[jax.ShapeDtypeStruct((T, LANES), F32), jax.ShapeDtypeStruct((T * ROW_CHUNKS, LANES), F32)],
        compiler_params=_params(("parallel",)),
        name="router",
    )(x, g, r_pad)


def _route_plan(i1, i2, tg, n_tiles):
    e = jnp.stack([i1, i2], axis=1).reshape(-1)
    oh = (e[:, None] == jnp.arange(N_EXPERTS, dtype=jnp.int32)[None, :]).astype(jnp.int32)
    csum = jnp.cumsum(oh, axis=0)
    counts = csum[-1]
    tiles_per = (counts + tg - 1) // tg
    tile_end = jnp.cumsum(tiles_per)
    tile_start = tile_end - tiles_per
    pos = (jnp.sum(oh * (csum + (tile_start * tg)[None, :]), axis=1) - 1).astype(jnp.int32)
    tile_ids = jnp.arange(n_tiles, dtype=jnp.int32)
    tile_expert = jnp.minimum(jnp.sum((tile_ids[:, None] >= tile_end[None, :]).astype(jnp.int32), axis=1),
                              N_EXPERTS - 1).astype(jnp.int32)
    tile_valid = (tile_ids < tile_end[-1]).astype(jnp.int32)
    return pos, tile_expert, tile_valid


def _moe_scatter_kernel(pos_ref, xt_hbm, init_hbm, xs_hbm, sem, *, tm):
    del init_hbm
    base = pl.program_id(0) * tm

    def issue(r, carry):
        t = base + r
        src = xt_hbm.at[pl.ds(pl.multiple_of(t * ROW_CHUNKS, ROW_CHUNKS), ROW_CHUNKS), :]
        for choice in range(2):
            p = pos_ref[2 * t + choice]
            dst = xs_hbm.at[pl.ds(pl.multiple_of(p * ROW_CHUNKS, ROW_CHUNKS), ROW_CHUNKS), :]
            pltpu.make_async_copy(src, dst, sem).start()
        return carry

    lax.fori_loop(0, tm, issue, 0, unroll=4)
    n = 2 * tm * ROW_CHUNKS
    pltpu.make_async_copy(xt_hbm.at[pl.ds(0, n), :], xs_hbm.at[pl.ds(0, n), :], sem).wait()


def _moe_scatter(xt, pos, n_rows):
    T = xt.shape[0] // ROW_CHUNKS
    tm = _tile(T, 640, LANES)
    init = jnp.zeros((n_rows * ROW_CHUNKS, LANES), F32)
    return pl.pallas_call(
        functools.partial(_moe_scatter_kernel, tm=tm),
        grid_spec=pltpu.PrefetchScalarGridSpec(
            num_scalar_prefetch=1,
            grid=(T // tm,),
            in_specs=[pl.BlockSpec(memory_space=pl.ANY), pl.BlockSpec(memory_space=pl.ANY)],
            out_specs=pl.BlockSpec(memory_space=pl.ANY),
            scratch_shapes=[pltpu.SemaphoreType.DMA(())]),
        out_shape=jax.ShapeDtypeStruct(init.shape, F32),
        input_output_aliases={2: 0},
        compiler_params=_params(("arbitrary",)),
        name="moe_scatter",
    )(pos, xt, init)


def _moe_group_kernel(te_ref, tv_ref, xs_ref, g_ref, w1_ref, w3_ref, w2_ref, o_ref, *, tg):
    del te_ref
    i = pl.program_id(0)

    @pl.when(tv_ref[i] > 0)
    def _():
        x = _from_row_tiles(xs_ref, tg)
        h = (x * lax.rsqrt(jnp.mean(x * x, axis=-1, keepdims=True) + EPS) * g_ref[...]).astype(BF16)
        acc = jnp.zeros((tg, D_MODEL), F32)
        for c in range(0, D_FF, MOE_FF_CHUNK):
            a = jnp.dot(h, w1_ref[:, c:c + MOE_FF_CHUNK], preferred_element_type=F32)
            g = jnp.dot(h, w3_ref[:, c:c + MOE_FF_CHUNK], preferred_element_type=F32)
            act = (a * jax.nn.sigmoid(a) * g).astype(BF16)
            acc = acc + jnp.dot(act, w2_ref[c:c + MOE_FF_CHUNK, :], preferred_element_type=F32)
        _to_row_tiles(o_ref, acc)

    @pl.when(tv_ref[i] == 0)
    def _():
        o_ref[...] = jnp.zeros((tg * ROW_CHUNKS, LANES), F32)


def _moe_group(xs, g, w1, w3, w2, tile_expert, tile_valid, tg, n_tiles):
    wspec = lambda shape: pl.BlockSpec((None,) + shape, lambda i, te, tv: (te[i], 0, 0))
    rows = pl.BlockSpec((tg * ROW_CHUNKS, LANES), lambda i, te, tv: (i, 0))
    return pl.pallas_call(
        functools.partial(_moe_group_kernel, tg=tg),
        grid_spec=pltpu.PrefetchScalarGridSpec(
            num_scalar_prefetch=2,
            grid=(n_tiles,),
            in_specs=[rows, pl.BlockSpec((1, D_MODEL), lambda i, te, tv: (0, 0)),
                      wspec((D_MODEL, D_FF)), wspec((D_MODEL, D_FF)), wspec((D_FF, D_MODEL))],
            out_specs=rows),
        out_shape=jax.ShapeDtypeStruct((n_tiles * tg * ROW_CHUNKS, LANES), F32),
        compiler_params=_params(("arbitrary",)),
        name="moe_group",
    )(tile_expert, tile_valid, xs, g, w1, w3, w2)


def _moe_combine_kernel(pos_ref, x_ref, route_ref, ys_hbm, o_ref, yb, sem, *, tm, n_steps):
    i = pl.program_id(0)

    def gather(step, slot):
        base = step * (2 * tm)

        def issue(r, carry):
            for choice in range(2):
                p = pos_ref[base + 2 * r + choice]
                src = ys_hbm.at[pl.ds(pl.multiple_of(p * ROW_CHUNKS, ROW_CHUNKS), ROW_CHUNKS), :]
                dst = yb.at[slot, choice, pl.ds(pl.multiple_of(r * ROW_CHUNKS, ROW_CHUNKS), ROW_CHUNKS), :]
                pltpu.make_async_copy(src, dst, sem.at[slot]).start()
            return carry

        lax.fori_loop(0, tm, issue, 0, unroll=4)

    @pl.when(i == 0)
    def _():
        gather(0, 0)

    @pl.when(i + 1 < n_steps)
    def _():
        gather(i + 1, (i + 1) % 2)

    slot = i % 2
    pltpu.make_async_copy(yb.at[slot], yb.at[slot], sem.at[slot]).wait()
    route = route_ref[...]
    g1 = route[:, ROUTE_G1:ROUTE_G1 + 1]
    g2 = route[:, ROUTE_G2:ROUTE_G2 + 1]
    o_ref[...] = x_ref[...] + g1 * _from_row_tiles(yb.at[slot, 0], tm) + g2 * _from_row_tiles(yb.at[slot, 1], tm)


def _moe_combine(x, route, ys, pos):
    T = x.shape[0]
    tm = _tile(T, 640, LANES)
    n_steps = T // tm
    row = lambda i, pos: (i, 0)
    return pl.pallas_call(
        functools.partial(_moe_combine_kernel, tm=tm, n_steps=n_steps),
        grid_spec=pltpu.PrefetchScalarGridSpec(
            num_scalar_prefetch=1,
            grid=(n_steps,),
            in_specs=[pl.BlockSpec((tm, D_MODEL), row), pl.BlockSpec((tm, LANES), row),
                      pl.BlockSpec(memory_space=pl.ANY)],
            out_specs=pl.BlockSpec((tm, D_MODEL), row),
            scratch_shapes=[pltpu.VMEM((2, 2, tm * ROW_CHUNKS, LANES), F32), pltpu.SemaphoreType.DMA((2,))]),
        out_shape=jax.ShapeDtypeStruct((T, D_MODEL), F32),
        compiler_params=_params(("arbitrary",)),
        name="moe_combine",
    )(pos, x, route, ys)


def _moe(x, g, r_pad, w1, w3, w2):
    T = x.shape[0]
    route, xt = _router(x, g, r_pad)
    i1 = route[:, ROUTE_I1].astype(jnp.int32)
    i2 = route[:, ROUTE_I2].astype(jnp.int32)
    tg = MOE_TILE
    n_tiles = pl.cdiv(2 * T, tg) + N_EXPERTS
    pos, tile_expert, tile_valid = _route_plan(i1, i2, tg, n_tiles)
    xs = _moe_scatter(xt, pos, n_tiles * tg)
    ys = _moe_group(xs, g, w1, w3, w2, tile_expert, tile_valid, tg, n_tiles)
    return _moe_combine(x, route, ys, pos)


def _rope_tables(pos):
    half = ROPE_DIM // 2
    inv_freq = ROPE_THETA ** (-2.0 * jnp.arange(half, dtype=F32) / ROPE_DIM)
    ang = pos[:, None] * inv_freq[None, :]
    cos, sin = jnp.cos(ang), jnp.sin(ang)
    n = pos.shape[0]
    ones = jnp.ones((n, HEAD_DIM - ROPE_DIM), F32)
    zeros = jnp.zeros((n, HEAD_DIM - ROPE_DIM), F32)
    zh = jnp.zeros((n, half), F32)
    c = jnp.concatenate([cos, cos, ones], axis=1)
    s1 = jnp.concatenate([zh, sin, zeros], axis=1)
    s2 = jnp.concatenate([-sin, zh, zeros], axis=1)
    two = lambda t: jnp.concatenate([t, t], axis=1)
    return two(c), two(s1), two(s2)


def _pair_state(c, n, m):
    nb = c.shape[0]
    c = c.reshape(nb, 2, 2, HEAD_DIM, HEAD_DIM)
    z = jnp.zeros((nb, 2, HEAD_DIM, HEAD_DIM), F32)
    top = jnp.concatenate([c[:, :, 0], z], axis=-1)
    bot = jnp.concatenate([z, c[:, :, 1]], axis=-1)
    cp = jnp.concatenate([top, bot], axis=-2)
    nm = jnp.zeros((nb, 8, LANES), F32)
    nm = nm.at[:, 0:2, :].set(n.reshape(nb, 2, LANES))
    nm = nm.at[:, 2, 0:N_HEADS_MLSTM].set(m)
    return cp, nm


def _unpair_state(cp, nm):
    nb = cp.shape[0]
    c = jnp.stack([cp[:, :, :HEAD_DIM, :HEAD_DIM], cp[:, :, HEAD_DIM:, HEAD_DIM:]], axis=2)
    c = c.reshape(nb, N_HEADS_MLSTM, HEAD_DIM, HEAD_DIM)
    n = nm[:, 0:2, :].reshape(nb, N_HEADS_MLSTM, HEAD_DIM)
    m = nm[:, 2, 0:N_HEADS_MLSTM]
    return c, n, m


def kernel(x_prompt, x_sample, cache_attn_k, cache_attn_v, state_mlstm_C, state_mlstm_n, state_mlstm_m, state_conv, norm_mix, w_in, q_norm, k_norm, mlstm_gate_bias, mlstm_out_norm, conv_dw_w, conv_dw_b, conv_norm_g, conv_norm_b, w_out, norm_ffn, ffn_w1, ffn_w3, ffn_w2, moe_router, moe_w1, moe_w3, moe_w2):
    batch, seq, _ = x_prompt.shape
    nbs, dec, _ = x_sample.shape
    depth = w_in.shape[0]
    tp = batch * seq
    keep = min(max(w for w, _ in DILATED_BRANCHES), seq)
    hist_rows = CONV_WIDTH - 1

    x = jnp.concatenate([x_prompt.reshape(tp, D_MODEL), x_sample.reshape(nbs * dec, D_MODEL)], axis=0)

    pos = jnp.concatenate([jnp.tile(jnp.arange(seq, dtype=F32), batch),
                           jnp.tile(PAST_LEN + jnp.arange(dec, dtype=F32), nbs)])
    rc, rs1, rs2 = _rope_tables(pos)
    gi = np.arange(D_ATTN) // HEAD_DIM
    bd = jnp.asarray((gi[:, None] == gi[None, :]).astype(np.float32) / HEAD_DIM, dtype=BF16)

    p_len = cache_attn_k.shape[2]
    cache_k = cache_attn_k.reshape(depth, nbs, p_len, D_ATTN)
    cache_v = cache_attn_v.reshape(depth, nbs, p_len, D_ATTN)

    zero_c = jnp.zeros((batch, 2, LANES, LANES), F32)
    zero_nm = jnp.zeros((batch, 8, LANES), F32)
    zero_hist = jnp.zeros((batch, HIST_PAD, D_CONV), F32)

    outs = [[] for _ in range(12)]
    for l in range(depth):
        o = 3 * D_ATTN + 4 * D_MLSTM
        w_p = jnp.concatenate([w_in[l][:, :o], jnp.pad(w_in[l][:, o:o + 8], ((0, 0), (0, GATE_PAD - 8))),
                               w_in[l][:, o + 8:]], axis=1).astype(BF16)
        qn_row = jnp.tile(q_norm[l], N_HEADS_ATTN)[None, :]
        kn_row = jnp.tile(k_norm[l], N_HEADS_ATTN)[None, :]
        gb_row = jnp.pad(mlstm_gate_bias[l], (0, GATE_PAD - 2 * N_HEADS_MLSTM))[None, :]

        q, k, v, zm, gt, u = _inproj(x, norm_mix[l][None, :], w_p, qn_row, kn_row, gb_row, bd, rc, rs1, rs2)

        att_p = _attn_prompt(q, k, v, batch, seq)
        on_row = mlstm_out_norm[l][None, :]
        hm_p, c1_p, nm1_p = _mlstm(zm, gt, zero_c, zero_nm, on_row, batch, seq)
        w_conv = jnp.pad(conv_dw_w[l], ((0, HIST_PAD - CONV_WIDTH), (0, 0)))
        conv_args = (w_conv, conv_dw_b[l][None, :], conv_norm_g[l][None, :], conv_norm_b[l][None, :], bd)
        cv_p, st_p = _conv(u, zero_hist, *conv_args, batch, seq)

        sq = lambda t: t[tp:].reshape(nbs, dec, t.shape[1])
        att_s = _attn_sample(sq(q), sq(k), sq(v), cache_k, cache_v, l)
        c0_s, nm0_s = _pair_state(state_mlstm_C[l], state_mlstm_n[l], state_mlstm_m[l])
        hm_s, c1_s, nm1_s = _mlstm(zm[tp:], gt[tp:], c0_s, nm0_s, on_row, nbs, dec)
        hist_s = jnp.pad(state_conv[l], ((0, 0), (HIST_PAD - hist_rows, 0), (0, 0)))
        cv_s, st_s = _conv(u[tp:], hist_s, *conv_args, nbs, dec)

        att = jnp.concatenate([att_p, att_s.reshape(nbs * dec, D_ATTN)], axis=0)
        hm = jnp.concatenate([hm_p, hm_s], axis=0)
        cv = jnp.concatenate([cv_p, cv_s], axis=0)
        x = _outproj(x, att, hm, cv, w_out[l].astype(BF16))

        i = l // 2
        if l % 2 == 0:
            x = _ffn(x, norm_ffn[l][None, :], ffn_w1[i].astype(BF16), ffn_w3[i].astype(BF16),
                     ffn_w2[i].astype(BF16))
        else:
            r_pad = jnp.pad(moe_router[i], ((0, 0), (0, LANES - N_EXPERTS)))
            x = _moe(x, norm_ffn[l][None, :], r_pad, moe_w1[i].astype(BF16), moe_w3[i].astype(BF16),
                     moe_w2[i].astype(BF16))

        kp = k[:tp].reshape(batch, seq, N_HEADS_ATTN, HEAD_DIM)[:, seq - keep:]
        vp = v[:tp].reshape(batch, seq, N_HEADS_ATTN, HEAD_DIM)[:, seq - keep:]
        cp_, np_, mp_ = _unpair_state(c1_p, nm1_p)
        cs_, ns_, ms_ = _unpair_state(c1_s, nm1_s)
        layer_outs = (kp, vp, sq(k).reshape(nbs, dec, N_HEADS_ATTN, HEAD_DIM),
                      sq(v).reshape(nbs, dec, N_HEADS_ATTN, HEAD_DIM),
                      cp_, np_, mp_, cs_, ns_, ms_,
                      st_p[:, HIST_PAD - hist_rows:], st_s[:, HIST_PAD - hist_rows:])
        for lst, val in zip(outs, layer_outs):
            lst.append(val)

    y_prompt = x[:tp].reshape(batch, seq, D_MODEL)
    y_sample = x[tp:].reshape(nbs, dec, D_MODEL)
    return (y_prompt, y_sample) + tuple(jnp.stack(lst, axis=0) for lst in outs)
```

```python
import functools

import numpy as np
import jax
import jax.numpy as jnp
from jax import lax
from jax.experimental import pallas as pl
from jax.experimental.pallas import tpu as pltpu

F32 = jnp.float32
BF16 = jnp.bfloat16

D_MODEL = 1024
HEAD_DIM = 64
N_HEADS_ATTN = 6
N_HEADS_MLSTM = 4
D_ATTN = N_HEADS_ATTN * HEAD_DIM
D_MLSTM = N_HEADS_MLSTM * HEAD_DIM
D_CONV = 384
DILATED_BRANCHES = ((128, 1), (512, 4), (2048, 16))
ATTN_BLOCK = 128
ROPE_DIM = HEAD_DIM // 4
ROPE_THETA = 500000.0
MLSTM_CHUNK = 128
CONV_WIDTH = 31
D_FF = 2816
N_EXPERTS = 8
EPS = 1e-6
NEG = -1e30
PAST_LEN = 16384

LANES = 128
SUBLANES = 8
ROW_CHUNKS = D_MODEL // LANES
HIST_PAD = 32
GATE_PAD = LANES
W_IN_COLS = 3 * D_ATTN + 4 * D_MLSTM + GATE_PAD + 2 * D_CONV
VMEM_LIMIT = 56 * 1024 * 1024
MOE_TILE = 512
MOE_FF_CHUNK = 256
ROUTE_I1, ROUTE_I2, ROUTE_G1, ROUTE_G2 = 8, 9, 10, 11


def _tile(n, pref, mult=8):
    for t in range(min(pref, n), 0, -1):
        if n % t == 0 and t % mult == 0:
            return t
    return n


def _params(sem):
    return pltpu.CompilerParams(dimension_semantics=sem, vmem_limit_bytes=VMEM_LIMIT)


def _split_dot(a, b_bf16):
    hi = a.astype(BF16)
    lo = (a - hi.astype(F32)).astype(BF16)
    return (jnp.dot(hi, b_bf16, preferred_element_type=F32)
            + jnp.dot(lo, b_bf16, preferred_element_type=F32))


def _group_mean(a, bd_ref):
    return _split_dot(a, bd_ref[...])


def _inproj_kernel(x_ref, g_ref, w_ref, qn_ref, kn_ref, gb_ref, bd_ref, rc_ref, rs1_ref, rs2_ref,
                   q_ref, k_ref, v_ref, zm_ref, gt_ref, u_ref):
    x = x_ref[...]
    h = (x * lax.rsqrt(jnp.mean(x * x, axis=-1, keepdims=True) + EPS) * g_ref[...]).astype(BF16)

    def proj(lo, hi):
        return jnp.dot(h, w_ref[:, lo:hi], preferred_element_type=F32)

    rc, rs1, rs2 = rc_ref[...], rs1_ref[...], rs2_ref[...]

    def norm_rope(z, gain_ref, scale):
        zn = z * lax.rsqrt(_group_mean(z * z, bd_ref) + EPS) * gain_ref[...]
        outs = []
        for c in range(D_ATTN // LANES):
            zc = zn[:, c * LANES:(c + 1) * LANES]
            y = (zc * rc + pltpu.roll(zc, ROPE_DIM // 2, 1) * rs1
                 + pltpu.roll(zc, LANES - ROPE_DIM // 2, 1) * rs2)
            outs.append(y * scale if scale != 1.0 else y)
        return jnp.concatenate(outs, axis=1)

    q_ref[...] = norm_rope(proj(0, D_ATTN), qn_ref, HEAD_DIM ** -0.5)
    k_ref[...] = norm_rope(proj(D_ATTN, 2 * D_ATTN), kn_ref, 1.0)
    v_ref[...] = proj(2 * D_ATTN, 3 * D_ATTN)

    o = 3 * D_ATTN
    zm = proj(o, o + 4 * D_MLSTM)
    lane = lax.broadcasted_iota(jnp.int32, (1, 4 * D_MLSTM), 1)
    zm_ref[...] = zm * jnp.where((lane >= D_MLSTM) & (lane < 2 * D_MLSTM), HEAD_DIM ** -0.5, 1.0)

    o += 4 * D_MLSTM
    gt = proj(o, o + GATE_PAD) + gb_ref[...]
    glane = lax.broadcasted_iota(jnp.int32, (1, GATE_PAD), 1)
    log_sig = jnp.minimum(gt, 0.0) - jnp.log1p(jnp.exp(-jnp.abs(gt)))
    gt_ref[...] = jnp.where(glane < N_HEADS_MLSTM, gt, log_sig)

    o += GATE_PAD
    cv = proj(o, o + D_CONV)
    cg = proj(o + D_CONV, o + 2 * D_CONV)
    u_ref[...] = cv * jax.nn.sigmoid(cg)


def _inproj(x, g, w_p, qn_row, kn_row, gb_row, bd, rc, rs1, rs2):
    T = x.shape[0]
    tm = _tile(T, 640, LANES)
    row = lambda i: (i, 0)
    const = lambda i: (0, 0)
    out_shapes = [jax.ShapeDtypeStruct((T, n), F32)
                  for n in (D_ATTN, D_ATTN, D_ATTN, 4 * D_MLSTM, GATE_PAD, D_CONV)]
    return pl.pallas_call(
        _inproj_kernel,
        grid=(T // tm,),
        in_specs=[pl.BlockSpec((tm, D_MODEL), row),
                  pl.BlockSpec((1, D_MODEL), const),
                  pl.BlockSpec((D_MODEL, W_IN_COLS), const),
                  pl.BlockSpec((1, D_ATTN), const),
                  pl.BlockSpec((1, D_ATTN), const),
                  pl.BlockSpec((1, GATE_PAD), const),
                  pl.BlockSpec((D_ATTN, D_ATTN), const),
                  pl.BlockSpec((tm, LANES), row),
                  pl.BlockSpec((tm, LANES), row),
                  pl.BlockSpec((tm, LANES), row)],
        out_specs=[pl.BlockSpec((tm, s.shape[1]), row) for s in out_shapes],
        out_shape=out_shapes,
        compiler_params=_params(("parallel",)),
        name="inproj",
    )(x, g, w_p, qn_row, kn_row, gb_row, bd, rc, rs1, rs2)


def _attn_prompt_kernel(q_ref, k_ref, v_ref, buf_ref, o_ref, qs, ks, vs, racc, rm, rden, nacc, nm, nden,
                        *, seq):
    del buf_ref
    blk = ATTN_BLOCK
    n_blocks = seq // blk
    lane = lax.broadcasted_iota(jnp.int32, (1, LANES), 1)
    head_a = lane < HEAD_DIM
    row = lax.broadcasted_iota(jnp.int32, (blk, 2 * blk), 0)
    col = lax.broadcasted_iota(jnp.int32, (blk, 2 * blk), 1)
    band = (col >= row) & (col <= row + blk)
    bias_band = jnp.where(band, 0.0, NEG)
    bias_first = jnp.where(band & (col >= blk), 0.0, NEG)

    ks[0:blk, :] = jnp.zeros((blk, LANES), BF16)
    vs[0:blk, :] = jnp.zeros((blk, LANES), BF16)

    cp = 256
    for (_, dil) in DILATED_BRANCHES:
        nsub = seq // dil
        nb = nsub // blk
        for r in range(dil):
            for c0 in range(0, nsub, cp):
                n = min(cp, nsub - c0)
                if dil == 1:
                    src = pl.ds(c0, n)
                else:
                    src = pl.ds(r + c0 * dil, n, stride=dil)
                dst = r * nsub + c0
                qs[dst:dst + n, :] = q_ref[src, :].astype(BF16)
                ks[blk + dst:blk + dst + n, :] = k_ref[src, :].astype(BF16)
                vs[blk + dst:blk + dst + n, :] = v_ref[src, :].astype(BF16)

        if dil == 1:
            dst_acc, dst_m, dst_den = nacc, nm, nden
        else:
            dst_acc, dst_m, dst_den = racc, rm, rden

        def body(j, carry, nb=nb, dst_acc=dst_acc, dst_m=dst_m, dst_den=dst_den):
            off = pl.multiple_of(j * blk, blk)
            qb = qs[pl.ds(off, blk), :]
            zero = jnp.zeros_like(qb)
            q2 = jnp.concatenate([jnp.where(head_a, qb, zero), jnp.where(head_a, zero, qb)], axis=0)
            kk = ks[pl.ds(off, 2 * blk), :]
            vv = vs[pl.ds(off, 2 * blk), :]
            s = lax.dot_general(q2, kk, (((1,), (1,)), ((), ())), preferred_element_type=F32)
            bias = jnp.where(j % nb == 0, bias_first, bias_band)
            s = s + jnp.concatenate([bias, bias], axis=0)
            mx = jnp.max(s, axis=1, keepdims=True)
            p = jnp.exp(s - mx)
            den = jnp.sum(p, axis=1, keepdims=True)
            o = jnp.dot(p.astype(BF16), vv, preferred_element_type=F32)
            dst_acc[pl.ds(off, blk), :] = jnp.where(head_a, o[:blk], o[blk:])
            dst_m[pl.ds(off, blk), :] = jnp.where(head_a, mx[:blk], mx[blk:])
            dst_den[pl.ds(off, blk), :] = jnp.where(head_a, den[:blk], den[blk:])
            return carry

        lax.fori_loop(0, n_blocks, body, 0, unroll=8)

        if dil > 1:
            for r in range(dil):
                for c0 in range(0, nsub, cp):
                    n = min(cp, nsub - c0)
                    nat = pl.ds(r + c0 * dil, n, stride=dil)
                    res = pl.ds(r * nsub + c0, n)
                    m_old, m_new = nm[nat, :], rm[res, :]
                    m_all = jnp.maximum(m_old, m_new)
                    e_old = jnp.exp(m_old - m_all)
                    e_new = jnp.exp(m_new - m_all)
                    nacc[nat, :] = e_old * nacc[nat, :] + e_new * racc[res, :]
                    nden[nat, :] = e_old * nden[nat, :] + e_new * rden[res, :]
                    nm[nat, :] = m_all

    o_ref[...] = nacc[...] / nden[...]


def _attn_prompt(q, k, v, buf, batch, seq):
    n_pairs = D_ATTN // LANES
    spec = pl.BlockSpec((seq, LANES), lambda b, hp: (b, hp))
    return pl.pallas_call(
        functools.partial(_attn_prompt_kernel, seq=seq),
        grid=(batch, n_pairs),
        in_specs=[spec, spec, spec, pl.BlockSpec(memory_space=pl.ANY)],
        out_specs=spec,
        out_shape=jax.ShapeDtypeStruct(buf.shape, F32),
        input_output_aliases={3: 0},
        scratch_shapes=[pltpu.VMEM((seq, LANES), BF16),
                        pltpu.VMEM((seq + ATTN_BLOCK, LANES), BF16),
                        pltpu.VMEM((seq + ATTN_BLOCK, LANES), BF16)]
                       + [pltpu.VMEM((seq, LANES), F32) for _ in range(6)],
        compiler_params=_params(("parallel", "parallel")),
        name="attn_prompt",
    )(q, k, v, buf)


def _attn_sample_kernel(q_ref, kn_ref, vn_ref, kc_ref, vc_ref, cw_ref, buf_ref, o_ref, kall, vall,
                        *, p_len, dec):
    del buf_ref
    pad_rows = kall.shape[0] - p_len
    kall[0:p_len, :] = kc_ref[...].astype(BF16)
    vall[0:p_len, :] = vc_ref[...].astype(BF16)
    tail = jnp.zeros((pad_rows - dec, D_ATTN), F32)
    kall[p_len:, :] = jnp.concatenate([kn_ref[...], tail], axis=0).astype(BF16)
    vall[p_len:, :] = jnp.concatenate([vn_ref[...], tail], axis=0).astype(BF16)

    lane = lax.broadcasted_iota(jnp.int32, (1, LANES), 1)
    head_a = lane < HEAD_DIM
    cw = cw_ref[...]
    cw2 = jnp.concatenate([cw, cw], axis=0)
    outs = []
    for hp in range(D_ATTN // LANES):
        sl = slice(hp * LANES, (hp + 1) * LANES)
        qb = q_ref[:, sl].astype(BF16)
        zero = jnp.zeros_like(qb)
        q2 = jnp.concatenate([jnp.where(head_a, qb, zero), jnp.where(head_a, zero, qb)], axis=0)
        s = lax.dot_general(q2, kall[:, sl], (((1,), (1,)), ((), ())), preferred_element_type=F32)
        s = jnp.where(cw2 > 0.0, s, NEG)
        mx = jnp.max(s, axis=1, keepdims=True)
        p = jnp.exp(s - mx) * cw2
        den = jnp.sum(p, axis=1, keepdims=True)
        o = jnp.dot(p.astype(BF16), vall[:, sl], preferred_element_type=F32) / den
        outs.append(jnp.where(head_a, o[:dec], o[dec:]))
    o_ref[...] = jnp.concatenate(outs, axis=1)


def _sample_multiplicity(p_len, dec, n_rows):
    t = np.arange(dec)[:, None]
    i = np.arange(n_rows)[None, :]
    dist = p_len + t - i
    cw = np.zeros((dec, n_rows), np.float32)
    for (w, d) in DILATED_BRANCHES:
        cw += ((dist >= 0) & (dist <= w) & (dist % d == 0) & (i < p_len + dec))
    return jnp.asarray(cw)


def _attn_sample(q, k, v, cache_k, cache_v, buf, layer, nb, dec, row0):
    p_len = cache_k.shape[2]
    n_rows = p_len + LANES
    cw = _sample_multiplicity(p_len, dec, n_rows)
    new = pl.BlockSpec((dec, D_ATTN), lambda b: (row0 // dec + b, 0))
    cache = pl.BlockSpec((None, None, p_len, D_ATTN), lambda b: (layer, b, 0, 0))
    return pl.pallas_call(
        functools.partial(_attn_sample_kernel, p_len=p_len, dec=dec),
        grid=(nb,),
        in_specs=[new, new, new, cache, cache, pl.BlockSpec((dec, n_rows), lambda b: (0, 0)),
                  pl.BlockSpec(memory_space=pl.ANY)],
        out_specs=new,
        out_shape=jax.ShapeDtypeStruct(buf.shape, F32),
        input_output_aliases={6: 0},
        scratch_shapes=[pltpu.VMEM((n_rows, D_ATTN), BF16), pltpu.VMEM((n_rows, D_ATTN), BF16)],
        compiler_params=_params(("parallel",)),
        name="attn_sample",
    )(q, k, v, cache_k, cache_v, cw, buf)


def _mlstm_kernel(zm_ref, gt_ref, c0_ref, nm0_ref, on_ref, buf_ref, h_ref, c1_ref, nm1_ref, c_s, nm_s,
                  *, rows):
    del buf_ref
    ch = MLSTM_CHUNK
    step = pl.program_id(1)

    @pl.when(step == 0)
    def _():
        c_s[...] = c0_ref[...]
        nm_s[...] = nm0_ref[...]

    def padded(a):
        if rows == ch:
            return a
        return jnp.concatenate([a, jnp.zeros((ch - rows, a.shape[1]), a.dtype)], axis=0)

    lane = lax.broadcasted_iota(jnp.int32, (1, LANES), 1)
    head_a = lane < HEAD_DIM
    ri = lax.broadcasted_iota(jnp.int32, (ch, ch), 0)
    ci = lax.broadcasted_iota(jnp.int32, (ch, ch), 1)
    causal = ci <= ri
    eye = ci == ri
    rowv = lax.broadcasted_iota(jnp.int32, (ch, 1), 0)
    valid = rowv < rows

    gt = padded(gt_ref[...])
    glane = lax.broadcasted_iota(jnp.int32, (1, LANES), 1)
    is_ig = glane < N_HEADS_MLSTM
    gt = jnp.where(valid, gt, jnp.where(is_ig, NEG, 0.0))
    lf_only = jnp.where(is_ig, 0.0, gt)
    fcum = jnp.dot(causal.astype(F32), lf_only, preferred_element_type=F32,
                   precision=lax.Precision.HIGHEST)

    nm = nm_s[...]
    m_row_new = nm[2:3, :]
    brow128 = lax.broadcasted_iota(jnp.int32, (LANES, 1), 0) < HEAD_DIM
    bd_mask = brow128 == head_a

    for pair in range(N_HEADS_MLSTM // 2):
        ls = slice(pair * LANES, (pair + 1) * LANES)
        qp = padded(zm_ref[:, ls])
        kp = padded(zm_ref[:, 2 * LANES + pair * LANES:2 * LANES + (pair + 1) * LANES])
        vp = padded(zm_ref[:, 4 * LANES + pair * LANES:4 * LANES + (pair + 1) * LANES])
        op = padded(zm_ref[:, 6 * LANES + pair * LANES:6 * LANES + (pair + 1) * LANES])
        kb, vb = kp.astype(BF16), vp.astype(BF16)
        c_pair = c_s[pair]
        cb = c_pair.astype(BF16)
        n_row = nm[pair:pair + 1, :]

        h_heads, wl_heads, decays = [], [], []
        for sub in range(2):
            hx = 2 * pair + sub
            mask = head_a if sub == 0 else jnp.logical_not(head_a)
            a = fcum[:, N_HEADS_MLSTM + hx:N_HEADS_MLSTM + hx + 1]
            igc = gt[:, hx:hx + 1]
            m0 = nm[2:3, hx:hx + 1]
            brow = jnp.sum(jnp.where(eye, igc - a, 0.0), axis=0, keepdims=True)
            dmat = jnp.where(causal, a + brow, NEG)
            inter = a + m0
            m = jnp.maximum(inter, jnp.max(dmat, axis=1, keepdims=True))
            w = jnp.exp(dmat - m)
            g = jnp.exp(inter - m)
            qx = jnp.where(mask, qp, 0.0)
            qxb = qx.astype(BF16)
            sc = lax.dot_general(qxb, kb, (((1,), (1,)), ((), ())), preferred_element_type=F32) * w
            num = (jnp.dot(sc.astype(BF16), vb, preferred_element_type=F32)
                   + g * jnp.dot(qxb, cb, preferred_element_type=F32))
            den = (jnp.sum(sc, axis=1, keepdims=True)
                   + g * jnp.sum(qx * n_row, axis=1, keepdims=True))
            h_heads.append(num / jnp.maximum(jnp.abs(den), jnp.exp(-m)))
            m_last = m[rows - 1:rows, :]
            f_last = a[rows - 1:rows, :]
            decays.append(jnp.exp(f_last + m0 - m_last))
            wl_heads.append(jnp.exp(f_last - a + igc - m_last))
            m_row_new = jnp.where(glane == hx, m_last, m_row_new)

        kw = kp * jnp.where(head_a, wl_heads[0], wl_heads[1])
        upd = lax.dot_general(kw.astype(BF16), vb, (((0,), (0,)), ((), ())), preferred_element_type=F32)
        c_s[pair] = jnp.where(bd_mask, jnp.where(brow128, decays[0], decays[1]) * c_pair + upd, 0.0)
        n_new = jnp.where(head_a, decays[0], decays[1]) * n_row + jnp.sum(kw, axis=0, keepdims=True)
        nm_s[pair:pair + 1, :] = n_new

        y = jax.nn.sigmoid(op) * jnp.where(head_a, h_heads[0], h_heads[1])
        y2 = y * y
        ss_a = jnp.sum(jnp.where(head_a, y2, 0.0), axis=1, keepdims=True)
        ss_b = jnp.sum(jnp.where(head_a, 0.0, y2), axis=1, keepdims=True)
        ms = jnp.where(head_a, ss_a, ss_b) * (1.0 / HEAD_DIM)
        out = y * lax.rsqrt(ms + EPS) * on_ref[:, ls]
        h_ref[:, ls] = out[:rows]

    nm_s[2:3, :] = m_row_new

    @pl.when(step == pl.num_programs(1) - 1)
    def _():
        c1_ref[...] = c_s[...]
        nm1_ref[...] = nm_s[...]


def _mlstm(zm, gt, c0, nm0, on_row, buf, nb, seq, row0):
    rows = min(seq, MLSTM_CHUNK)
    n_chunks = seq // rows
    tok = lambda b, c: (row0 // rows + b * n_chunks + c, 0)
    state = lambda b, c: (b, 0, 0, 0)
    return pl.pallas_call(
        functools.partial(_mlstm_kernel, rows=rows),
        grid=(nb, n_chunks),
        in_specs=[pl.BlockSpec((rows, 4 * D_MLSTM), tok),
                  pl.BlockSpec((rows, GATE_PAD), tok),
                  pl.BlockSpec((None, 2, LANES, LANES), state),
                  pl.BlockSpec((None, 8, LANES), lambda b, c: (b, 0, 0)),
                  pl.BlockSpec((1, D_MLSTM), lambda b, c: (0, 0)),
                  pl.BlockSpec(memory_space=pl.ANY)],
        out_specs=[pl.BlockSpec((rows, D_MLSTM), tok),
                   pl.BlockSpec((None, 2, LANES, LANES), state),
                   pl.BlockSpec((None, 8, LANES), lambda b, c: (b, 0, 0))],
        out_shape=[jax.ShapeDtypeStruct(buf.shape, F32),
                   jax.ShapeDtypeStruct((nb, 2, LANES, LANES), F32),
                   jax.ShapeDtypeStruct((nb, 8, LANES), F32)],
        input_output_aliases={5: 0},
        scratch_shapes=[pltpu.VMEM((2, LANES, LANES), F32), pltpu.VMEM((8, LANES), F32)],
        compiler_params=_params(("parallel", "arbitrary")),
        name="mlstm",
    )(zm, gt, c0, nm0, on_row, buf)


def _conv_kernel(u_ref, hist_ref, w_ref, b_ref, g_ref, nb_ref, bd_ref, buf_ref, c_ref, st_ref, uc,
                 *, seq, rt):
    del buf_ref
    lead = HIST_PAD - (CONV_WIDTH - 1)
    uc[0:HIST_PAD, :] = hist_ref[...]
    uc[HIST_PAD:HIST_PAD + seq, :] = u_ref[...]
    uc[HIST_PAD + seq:, :] = jnp.zeros((SUBLANES, D_CONV), F32)
    w = w_ref[...]

    def chunk(c, carry):
        base = pl.multiple_of(c * rt, rt)
        win = uc[pl.ds(base, rt + HIST_PAD + SUBLANES), :]
        acc = None
        for b in range(SUBLANES):
            inner = None
            for a in range((lead + CONV_WIDTH - 1) // SUBLANES + 1):
                j = SUBLANES * a + b - lead
                if 0 <= j < CONV_WIDTH:
                    term = win[SUBLANES * a:SUBLANES * a + rt + SUBLANES, :] * w[j:j + 1, :]
                    inner = term if inner is None else inner + term
            part = inner[b:b + rt, :]
            acc = part if acc is None else acc + part
        y = acc + b_ref[...]
        mu = _group_mean(y, bd_ref)
        yc = y - mu
        var = _group_mean(yc * yc, bd_ref)
        z = yc * lax.rsqrt(var + EPS) * g_ref[...] + nb_ref[...]
        c_ref[pl.ds(base, rt), :] = z * jax.nn.sigmoid(z)
        return carry

    lax.fori_loop(0, seq // rt, chunk, 0)
    st_ref[...] = uc[seq:seq + HIST_PAD, :]


def _conv(u, hist, w, b, g, nbias, bd, buf, nb, seq, row0):
    rt = min(seq, 256)
    tok = lambda i: (row0 // seq + i, 0)
    const = lambda i: (0, 0)
    hspec = pl.BlockSpec((None, HIST_PAD, D_CONV), lambda i: (i, 0, 0))
    return pl.pallas_call(
        functools.partial(_conv_kernel, seq=seq, rt=rt),
        grid=(nb,),
        in_specs=[pl.BlockSpec((seq, D_CONV), tok), hspec,
                  pl.BlockSpec((HIST_PAD, D_CONV), const),
                  pl.BlockSpec((1, D_CONV), const), pl.BlockSpec((1, D_CONV), const),
                  pl.BlockSpec((1, D_CONV), const), pl.BlockSpec((D_CONV, D_CONV), const),
                  pl.BlockSpec(memory_space=pl.ANY)],
        out_specs=[pl.BlockSpec((seq, D_CONV), tok), hspec],
        out_shape=[jax.ShapeDtypeStruct(buf.shape, F32),
                   jax.ShapeDtypeStruct((nb, HIST_PAD, D_CONV), F32)],
        input_output_aliases={7: 0},
        scratch_shapes=[pltpu.VMEM((seq + HIST_PAD + SUBLANES, D_CONV), F32)],
        compiler_params=_params(("parallel",)),
        name="conv",
    )(u, hist, w, b, g, nbias, bd, buf)


def _outproj_kernel(x_ref, a_ref, m_ref, c_ref, w_ref, o_ref):
    acc = jnp.dot(a_ref[...].astype(BF16), w_ref[0:D_ATTN, :], preferred_element_type=F32)
    acc += jnp.dot(m_ref[...].astype(BF16), w_ref[D_ATTN:D_ATTN + D_MLSTM, :], preferred_element_type=F32)
    acc += jnp.dot(c_ref[...].astype(BF16), w_ref[D_ATTN + D_MLSTM:, :], preferred_element_type=F32)
    o_ref[...] = x_ref[...] + acc


def _outproj(x, att, mo, cv, w_out):
    T = x.shape[0]
    tm = _tile(T, 640, LANES)
    row = lambda i: (i, 0)
    return pl.pallas_call(
        _outproj_kernel,
        grid=(T // tm,),
        in_specs=[pl.BlockSpec((tm, D_MODEL), row), pl.BlockSpec((tm, D_ATTN), row),
                  pl.BlockSpec((tm, D_MLSTM), row), pl.BlockSpec((tm, D_CONV), row),
                  pl.BlockSpec((D_MODEL, D_MODEL), lambda i: (0, 0))],
        out_specs=pl.BlockSpec((tm, D_MODEL), row),
        out_shape=jax.ShapeDtypeStruct((T, D_MODEL), F32),
        compiler_params=_params(("parallel",)),
        name="outproj",
    )(x, att, mo, cv, w_out)


def _ffn_kernel(x_ref, g_ref, w1_ref, w3_ref, w2_ref, o_ref, h_s, acc_s):
    j = pl.program_id(1)

    @pl.when(j == 0)
    def _():
        x = x_ref[...]
        h_s[...] = (x * lax.rsqrt(jnp.mean(x * x, axis=-1, keepdims=True) + EPS) * g_ref[...]).astype(BF16)
        acc_s[...] = x

    h = h_s[...]
    a = jnp.dot(h, w1_ref[...], preferred_element_type=F32)
    g = jnp.dot(h, w3_ref[...], preferred_element_type=F32)
    act = (a * jax.nn.sigmoid(a) * g).astype(BF16)
    acc_s[...] += jnp.dot(act, w2_ref[...], preferred_element_type=F32)

    @pl.when(j == pl.num_programs(1) - 1)
    def _():
        o_ref[...] = acc_s[...]


def _ffn(x, g, w1, w3, w2):
    T = x.shape[0]
    tm = _tile(T, 640, LANES)
    tf = D_FF // 2
    row = lambda i, j: (i, 0)
    return pl.pallas_call(
        _ffn_kernel,
        grid=(T // tm, D_FF // tf),
        in_specs=[pl.BlockSpec((tm, D_MODEL), row), pl.BlockSpec((1, D_MODEL), lambda i, j: (0, 0)),
                  pl.BlockSpec((D_MODEL, tf), lambda i, j: (0, j)),
                  pl.BlockSpec((D_MODEL, tf), lambda i, j: (0, j)),
                  pl.BlockSpec((tf, D_MODEL), lambda i, j: (j, 0))],
        out_specs=pl.BlockSpec((tm, D_MODEL), row),
        out_shape=jax.ShapeDtypeStruct((T, D_MODEL), F32),
        scratch_shapes=[pltpu.VMEM((tm, D_MODEL), BF16), pltpu.VMEM((tm, D_MODEL), F32)],
        compiler_params=_params(("parallel", "arbitrary")),
        name="ffn",
    )(x, g, w1, w3, w2)


def _to_row_tiles(ref, val):
    n = val.shape[0]
    for c in range(ROW_CHUNKS):
        ref[pl.ds(c, n, stride=ROW_CHUNKS), :] = val[:, c * LANES:(c + 1) * LANES]


def _from_row_tiles(ref, n):
    return jnp.concatenate([ref[pl.ds(c, n, stride=ROW_CHUNKS), :] for c in range(ROW_CHUNKS)], axis=1)


def _router_kernel(x_ref, g_ref, r_ref, comb_ref, xt_ref):
    x = x_ref[...]
    _to_row_tiles(xt_ref, x)
    h = x * lax.rsqrt(jnp.mean(x * x, axis=-1, keepdims=True) + EPS) * g_ref[...]
    logits = jnp.dot(h, r_ref[...], preferred_element_type=F32, precision=lax.Precision.HIGHEST)
    lane = lax.broadcasted_iota(jnp.int32, logits.shape, 1)
    logits = jnp.where(lane < N_EXPERTS, logits, NEG)
    v1 = jnp.max(logits, axis=1, keepdims=True)
    i1 = jnp.min(jnp.where(logits == v1, lane, LANES), axis=1, keepdims=True)
    rest = jnp.where(lane == i1, NEG, logits)
    v2 = jnp.max(rest, axis=1, keepdims=True)
    i2 = jnp.min(jnp.where(rest == v2, lane, LANES), axis=1, keepdims=True)
    e2 = jnp.exp(v2 - v1)
    inv = 1.0 / (1.0 + e2)
    route = jnp.where(lane == ROUTE_I1, i1.astype(F32), jnp.where(lane == ROUTE_I2, i2.astype(F32), 0.0))
    comb_ref[...] = jnp.where(lane == ROUTE_G1, inv, jnp.where(lane == ROUTE_G2, e2 * inv, route))


def _router(x, g, r_pad):
    T = x.shape[0]
    tm = _tile(T, 640, LANES)
    row = lambda i: (i, 0)
    return pl.pallas_call(
        _router_kernel,
        grid=(T // tm,),
        in_specs=[pl.BlockSpec((tm, D_MODEL), row), pl.BlockSpec((1, D_MODEL), lambda i: (0, 0)),
                  pl.BlockSpec((D_MODEL, LANES), lambda i: (0, 0))],
        out_specs=[pl.BlockSpec((tm, LANES), row), pl.BlockSpec((tm * ROW_CHUNKS, LANES), row)],
        out_shape=[jax.ShapeDtypeStruct((T, LANES), F32), jax.ShapeDtypeStruct((T * ROW_CHUNKS, LANES), F32)],
        compiler_params=_params(("parallel",)),
        name="router",
    )(x, g, r_pad)


def _route_plan(i1, i2, tg, n_tiles):
    e = jnp.stack([i1, i2], axis=1).reshape(-1)
    oh = (e[:, None] == jnp.arange(N_EXPERTS, dtype=jnp.int32)[None, :]).astype(jnp.int32)
    csum = jnp.cumsum(oh, axis=0)
    counts = csum[-1]
    tiles_per = (counts + tg - 1) // tg
    tile_end = jnp.cumsum(tiles_per)
    tile_start = tile_end - tiles_per
    pos = (jnp.sum(oh * (csum + (tile_start * tg)[None, :]), axis=1) - 1).astype(jnp.int32)
    tile_ids = jnp.arange(n_tiles, dtype=jnp.int32)
    tile_expert = jnp.minimum(jnp.sum((tile_ids[:, None] >= tile_end[None, :]).astype(jnp.int32), axis=1),
                              N_EXPERTS - 1).astype(jnp.int32)
    tile_valid = (tile_ids < tile_end[-1]).astype(jnp.int32)
    return pos, tile_expert, tile_valid


def _moe_scatter_kernel(pos_ref, xt_ref, init_hbm, xs_hbm, sem, *, tm):
    del init_hbm
    base = pl.program_id(0) * tm

    def issue(r, carry):
        src = xt_ref.at[pl.ds(pl.multiple_of(r * ROW_CHUNKS, ROW_CHUNKS), ROW_CHUNKS), :]
        for choice in range(2):
            p = pos_ref[2 * (base + r) + choice]
            dst = xs_hbm.at[pl.ds(pl.multiple_of(p * ROW_CHUNKS, ROW_CHUNKS), ROW_CHUNKS), :]
            pltpu.make_async_copy(src, dst, sem.at[choice]).start(priority=choice)
        return carry

    lax.fori_loop(0, tm, issue, 0, unroll=4)
    for choice in range(2):
        pltpu.make_async_copy(xt_ref, xs_hbm.at[pl.ds(0, tm * ROW_CHUNKS), :], sem.at[choice]).wait()


def _moe_scatter(xt, pos, n_rows):
    T = xt.shape[0] // ROW_CHUNKS
    tm = _tile(T, 640, LANES)
    init = jnp.zeros((n_rows * ROW_CHUNKS, LANES), F32)
    return pl.pallas_call(
        functools.partial(_moe_scatter_kernel, tm=tm),
        grid_spec=pltpu.PrefetchScalarGridSpec(
            num_scalar_prefetch=1,
            grid=(T // tm,),
            in_specs=[pl.BlockSpec((tm * ROW_CHUNKS, LANES), lambda i, pos: (i, 0)),
                      pl.BlockSpec(memory_space=pl.ANY)],
            out_specs=pl.BlockSpec(memory_space=pl.ANY),
            scratch_shapes=[pltpu.SemaphoreType.DMA((2,))]),
        out_shape=jax.ShapeDtypeStruct(init.shape, F32),
        input_output_aliases={2: 0},
        compiler_params=_params(("arbitrary",)),
        name="moe_scatter",
    )(pos, xt, init)


def _moe_group_kernel(te_ref, tv_ref, xs_ref, g_ref, w1_ref, w3_ref, w2_ref, o_ref, *, tg):
    del te_ref
    i = pl.program_id(0)

    @pl.when(tv_ref[i] > 0)
    def _():
        x = _from_row_tiles(xs_ref, tg)
        h = (x * lax.rsqrt(jnp.mean(x * x, axis=-1, keepdims=True) + EPS) * g_ref[...]).astype(BF16)
        acc = jnp.zeros((tg, D_MODEL), F32)
        for c in range(0, D_FF, MOE_FF_CHUNK):
            a = jnp.dot(h, w1_ref[:, c:c + MOE_FF_CHUNK], preferred_element_type=F32)
            g = jnp.dot(h, w3_ref[:, c:c + MOE_FF_CHUNK], preferred_element_type=F32)
            act = (a * jax.nn.sigmoid(a) * g).astype(BF16)
            acc = acc + jnp.dot(act, w2_ref[c:c + MOE_FF_CHUNK, :], preferred_element_type=F32)
        _to_row_tiles(o_ref, acc)

    @pl.when(tv_ref[i] == 0)
    def _():
        o_ref[...] = jnp.zeros((tg * ROW_CHUNKS, LANES), F32)


def _moe_group(xs, g, w1, w3, w2, tile_expert, tile_valid, tg, n_tiles):
    wspec = lambda shape: pl.BlockSpec((None,) + shape, lambda i, te, tv: (te[i], 0, 0))
    rows = pl.BlockSpec((tg * ROW_CHUNKS, LANES), lambda i, te, tv: (i, 0))
    return pl.pallas_call(
        functools.partial(_moe_group_kernel, tg=tg),
        grid_spec=pltpu.PrefetchScalarGridSpec(
            num_scalar_prefetch=2,
            grid=(n_tiles,),
            in_specs=[rows, pl.BlockSpec((1, D_MODEL), lambda i, te, tv: (0, 0)),
                      wspec((D_MODEL, D_FF)), wspec((D_MODEL, D_FF)), wspec((D_FF, D_MODEL))],
            out_specs=rows),
        out_shape=jax.ShapeDtypeStruct((n_tiles * tg * ROW_CHUNKS, LANES), F32),
        compiler_params=_params(("arbitrary",)),
        name="moe_group",
    )(tile_expert, tile_valid, xs, g, w1, w3, w2)


def _moe_combine_kernel(pos_ref, x_ref, route_ref, ys_hbm, o_ref, yb, sem, *, tm, n_steps):
    i = pl.program_id(0)

    def gather(step, slot):
        base = step * (2 * tm)

        def issue(r, carry):
            for choice in range(2):
                p = pos_ref[base + 2 * r + choice]
                src = ys_hbm.at[pl.ds(pl.multiple_of(p * ROW_CHUNKS, ROW_CHUNKS), ROW_CHUNKS), :]
                dst = yb.at[slot, choice, pl.ds(pl.multiple_of(r * ROW_CHUNKS, ROW_CHUNKS), ROW_CHUNKS), :]
                pltpu.make_async_copy(src, dst, sem.at[slot]).start(priority=choice)
            return carry

        lax.fori_loop(0, tm, issue, 0, unroll=4)

    @pl.when(i == 0)
    def _():
        gather(0, 0)

    @pl.when(i + 1 < n_steps)
    def _():
        gather(i + 1, (i + 1) % 2)

    slot = i % 2
    pltpu.make_async_copy(yb.at[slot], yb.at[slot], sem.at[slot]).wait()
    route = route_ref[...]
    g1 = route[:, ROUTE_G1:ROUTE_G1 + 1]
    g2 = route[:, ROUTE_G2:ROUTE_G2 + 1]
    o_ref[...] = x_ref[...] + g1 * _from_row_tiles(yb.at[slot, 0], tm) + g2 * _from_row_tiles(yb.at[slot, 1], tm)


def _moe_combine(x, route, ys, pos):
    T = x.shape[0]
    tm = _tile(T, 640, LANES)
    n_steps = T // tm
    row = lambda i, pos: (i, 0)
    return pl.pallas_call(
        functools.partial(_moe_combine_kernel, tm=tm, n_steps=n_steps),
        grid_spec=pltpu.PrefetchScalarGridSpec(
            num_scalar_prefetch=1,
            grid=(n_steps,),
            in_specs=[pl.BlockSpec((tm, D_MODEL), row), pl.BlockSpec((tm, LANES), row),
                      pl.BlockSpec(memory_space=pl.ANY)],
            out_specs=pl.BlockSpec((tm, D_MODEL), row),
            scratch_shapes=[pltpu.VMEM((2, 2, tm * ROW_CHUNKS, LANES), F32), pltpu.SemaphoreType.DMA((2,))]),
        out_shape=jax.ShapeDtypeStruct((T, D_MODEL), F32),
        compiler_params=_params(("arbitrary",)),
        name="moe_combine",
    )(pos, x, route, ys)


def _moe(x, g, r_pad, w1, w3, w2):
    T = x.shape[0]
    route, xt = _router(x, g, r_pad)
    i1 = route[:, ROUTE_I1].astype(jnp.int32)
    i2 = route[:, ROUTE_I2].astype(jnp.int32)
    tg = MOE_TILE
    n_tiles = pl.cdiv(2 * T, tg) + N_EXPERTS
    pos, tile_expert, tile_valid = _route_plan(i1, i2, tg, n_tiles)
    xs = _moe_scatter(xt, pos, n_tiles * tg)
    ys = _moe_group(xs, g, w1, w3, w2, tile_expert, tile_valid, tg, n_tiles)
    return _moe_combine(x, route, ys, pos)


def _rope_tables(pos):
    half = ROPE_DIM // 2
    inv_freq = ROPE_THETA ** (-2.0 * jnp.arange(half, dtype=F32) / ROPE_DIM)
    ang = pos[:, None] * inv_freq[None, :]
    cos, sin = jnp.cos(ang), jnp.sin(ang)
    n = pos.shape[0]
    ones = jnp.ones((n, HEAD_DIM - ROPE_DIM), F32)
    zeros = jnp.zeros((n, HEAD_DIM - ROPE_DIM), F32)
    zh = jnp.zeros((n, half), F32)
    c = jnp.concatenate([cos, cos, ones], axis=1)
    s1 = jnp.concatenate([zh, sin, zeros], axis=1)
    s2 = jnp.concatenate([-sin, zh, zeros], axis=1)
    two = lambda t: jnp.concatenate([t, t], axis=1)
    return two(c), two(s1), two(s2)


def _pair_state(c, n, m):
    nb = c.shape[0]
    c = c.reshape(nb, 2, 2, HEAD_DIM, HEAD_DIM)
    z = jnp.zeros((nb, 2, HEAD_DIM, HEAD_DIM), F32)
    top = jnp.concatenate([c[:, :, 0], z], axis=-1)
    bot = jnp.concatenate([z, c[:, :, 1]], axis=-1)
    cp = jnp.concatenate([top, bot], axis=-2)
    nm = jnp.zeros((nb, 8, LANES), F32)
    nm = nm.at[:, 0:2, :].set(n.reshape(nb, 2, LANES))
    nm = nm.at[:, 2, 0:N_HEADS_MLSTM].set(m)
    return cp, nm


def _unpair_state(cp, nm):
    nb = cp.shape[0]
    c = jnp.stack([cp[:, :, :HEAD_DIM, :HEAD_DIM], cp[:, :, HEAD_DIM:, HEAD_DIM:]], axis=2)
    c = c.reshape(nb, N_HEADS_MLSTM, HEAD_DIM, HEAD_DIM)
    n = nm[:, 0:2, :].reshape(nb, N_HEADS_MLSTM, HEAD_DIM)
    m = nm[:, 2, 0:N_HEADS_MLSTM]
    return c, n, m


def kernel(x_prompt, x_sample, cache_attn_k, cache_attn_v, state_mlstm_C, state_mlstm_n, state_mlstm_m, state_conv, norm_mix, w_in, q_norm, k_norm, mlstm_gate_bias, mlstm_out_norm, conv_dw_w, conv_dw_b, conv_norm_g, conv_norm_b, w_out, norm_ffn, ffn_w1, ffn_w3, ffn_w2, moe_router, moe_w1, moe_w3, moe_w2):
    batch, seq, _ = x_prompt.shape
    nbs, dec, _ = x_sample.shape
    depth = w_in.shape[0]
    tp = batch * seq
    T = tp + nbs * dec
    keep = min(max(w for w, _ in DILATED_BRANCHES), seq)
    hist_rows = CONV_WIDTH - 1

    x = jnp.concatenate([x_prompt.reshape(tp, D_MODEL), x_sample.reshape(nbs * dec, D_MODEL)], axis=0)

    pos = jnp.concatenate([jnp.tile(jnp.arange(seq, dtype=F32), batch),
                           jnp.tile(PAST_LEN + jnp.arange(dec, dtype=F32), nbs)])
    rc, rs1, rs2 = _rope_tables(pos)
    gi = np.arange(D_ATTN) // HEAD_DIM
    bd = jnp.asarray((gi[:, None] == gi[None, :]).astype(np.float32) / HEAD_DIM, dtype=BF16)

    p_len = cache_attn_k.shape[2]
    cache_k = cache_attn_k.reshape(depth, nbs, p_len, D_ATTN)
    cache_v = cache_attn_v.reshape(depth, nbs, p_len, D_ATTN)

    zero_c = jnp.zeros((batch, 2, LANES, LANES), F32)
    zero_nm = jnp.zeros((batch, 8, LANES), F32)
    zero_hist = jnp.zeros((batch, HIST_PAD, D_CONV), F32)

    outs = [[] for _ in range(12)]
    for l in range(depth):
        o = 3 * D_ATTN + 4 * D_MLSTM
        w_p = jnp.concatenate([w_in[l][:, :o], jnp.pad(w_in[l][:, o:o + 8], ((0, 0), (0, GATE_PAD - 8))),
                               w_in[l][:, o + 8:]], axis=1).astype(BF16)
        qn_row = jnp.tile(q_norm[l], N_HEADS_ATTN)[None, :]
        kn_row = jnp.tile(k_norm[l], N_HEADS_ATTN)[None, :]
        gb_row = jnp.pad(mlstm_gate_bias[l], (0, GATE_PAD - 2 * N_HEADS_MLSTM))[None, :]

        q, k, v, zm, gt, u = _inproj(x, norm_mix[l][None, :], w_p, qn_row, kn_row, gb_row, bd, rc, rs1, rs2)

        on_row = mlstm_out_norm[l][None, :]
        w_conv = jnp.pad(conv_dw_w[l], ((0, HIST_PAD - CONV_WIDTH), (0, 0)))
        conv_args = (w_conv, conv_dw_b[l][None, :], conv_norm_g[l][None, :], conv_norm_b[l][None, :], bd)

        att = _attn_sample(q, k, v, cache_k, cache_v, jnp.zeros((T, D_ATTN), F32), l, nbs, dec, tp)
        c0_s, nm0_s = _pair_state(state_mlstm_C[l], state_mlstm_n[l], state_mlstm_m[l])
        hm, c1_s, nm1_s = _mlstm(zm, gt, c0_s, nm0_s, on_row, jnp.zeros((T, D_MLSTM), F32), nbs, dec, tp)
        hist_s = jnp.pad(state_conv[l], ((0, 0), (HIST_PAD - hist_rows, 0), (0, 0)))
        cv, st_s = _conv(u, hist_s, *conv_args, jnp.zeros((T, D_CONV), F32), nbs, dec, tp)

        att = _attn_prompt(q, k, v, att, batch, seq)
        hm, c1_p, nm1_p = _mlstm(zm, gt, zero_c, zero_nm, on_row, hm, batch, seq, 0)
        cv, st_p = _conv(u, zero_hist, *conv_args, cv, batch, seq, 0)

        x = _outproj(x, att, hm, cv, w_out[l].astype(BF16))

        i = l // 2
        if l % 2 == 0:
            x = _ffn(x, norm_ffn[l][None, :], ffn_w1[i].astype(BF16), ffn_w3[i].astype(BF16),
                     ffn_w2[i].astype(BF16))
        else:
            r_pad = jnp.pad(moe_router[i], ((0, 0), (0, LANES - N_EXPERTS)))
            x = _moe(x, norm_ffn[l][None, :], r_pad, moe_w1[i].astype(BF16), moe_w3[i].astype(BF16),
                     moe_w2[i].astype(BF16))

        kp = k[:tp].reshape(batch, seq, N_HEADS_ATTN, HEAD_DIM)[:, seq - keep:]
        vp = v[:tp].reshape(batch, seq, N_HEADS_ATTN, HEAD_DIM)[:, seq - keep:]
        cp_, np_, mp_ = _unpair_state(c1_p, nm1_p)
        cs_, ns_, ms_ = _unpair_state(c1_s, nm1_s)
        layer_outs = (kp, vp, k[tp:].reshape(nbs, dec, N_HEADS_ATTN, HEAD_DIM),
                      v[tp:].reshape(nbs, dec, N_HEADS_ATTN, HEAD_DIM),
                      cp_, np_, mp_, cs_, ns_, ms_,
                      st_p[:, HIST_PAD - hist_rows:], st_s[:, HIST_PAD - hist_rows:])
        for lst, val in zip(outs, layer_outs):
            lst.append(val)

    y_prompt = x[:tp].reshape(batch, seq, D_MODEL)
    y_sample = x[tp:].reshape(nbs, dec, D_MODEL)
    return (y_prompt, y_sample) + tuple(jnp.stack(lst, axis=0) for lst in outs)
```

```python
import functools

import numpy as np
import jax
import jax.numpy as jnp
from jax import lax
from jax.experimental import pallas as pl
from jax.experimental.pallas import tpu as pltpu

F32 = jnp.float32
BF16 = jnp.bfloat16

D_MODEL = 1024
HEAD_DIM = 64
N_HEADS_ATTN = 6
N_HEADS_MLSTM = 4
D_ATTN = N_HEADS_ATTN * HEAD_DIM
D_MLSTM = N_HEADS_MLSTM * HEAD_DIM
D_CONV = 384
DILATED_BRANCHES = ((128, 1), (512, 4), (2048, 16))
ATTN_BLOCK = 128
ROPE_DIM = HEAD_DIM // 4
ROPE_THETA = 500000.0
MLSTM_CHUNK = 128
CONV_WIDTH = 31
D_FF = 2816
N_EXPERTS = 8
EPS = 1e-6
NEG = -1e30
PAST_LEN = 16384

LANES = 128
SUBLANES = 8
ROW_CHUNKS = D_MODEL // LANES
HIST_PAD = 32
GATE_PAD = LANES
W_IN_COLS = 3 * D_ATTN + 4 * D_MLSTM + GATE_PAD + 2 * D_CONV
VMEM_LIMIT = 56 * 1024 * 1024
MOE_TILE = 512
MOE_FF_CHUNK = 256
ROUTE_I1, ROUTE_I2, ROUTE_G1, ROUTE_G2 = 8, 9, 10, 11


def _tile(n, pref, mult=8):
    for t in range(min(pref, n), 0, -1):
        if n % t == 0 and t % mult == 0:
            return t
    return n


def _params(sem):
    return pltpu.CompilerParams(dimension_semantics=sem, vmem_limit_bytes=VMEM_LIMIT)


def _split_dot(a, b_bf16):
    hi = a.astype(BF16)
    lo = (a - hi.astype(F32)).astype(BF16)
    return (jnp.dot(hi, b_bf16, preferred_element_type=F32)
            + jnp.dot(lo, b_bf16, preferred_element_type=F32))


def _group_mean(a, bd_ref):
    return _split_dot(a, bd_ref[...])


def _inproj_kernel(x_ref, g_ref, w_ref, qn_ref, kn_ref, gb_ref, bd_ref, rc_ref, rs1_ref, rs2_ref,
                   q_ref, k_ref, v_ref, zm_ref, gt_ref, u_ref):
    x = x_ref[...]
    h = (x * lax.rsqrt(jnp.mean(x * x, axis=-1, keepdims=True) + EPS) * g_ref[...]).astype(BF16)

    def proj(lo, hi):
        return jnp.dot(h, w_ref[:, lo:hi], preferred_element_type=F32)

    rc, rs1, rs2 = rc_ref[...], rs1_ref[...], rs2_ref[...]

    def norm_rope(z, gain_ref, scale):
        zn = z * lax.rsqrt(_group_mean(z * z, bd_ref) + EPS) * gain_ref[...]
        outs = []
        for c in range(D_ATTN // LANES):
            zc = zn[:, c * LANES:(c + 1) * LANES]
            y = (zc * rc + pltpu.roll(zc, ROPE_DIM // 2, 1) * rs1
                 + pltpu.roll(zc, LANES - ROPE_DIM // 2, 1) * rs2)
            outs.append(y * scale if scale != 1.0 else y)
        return jnp.concatenate(outs, axis=1)

    q_ref[...] = norm_rope(proj(0, D_ATTN), qn_ref, HEAD_DIM ** -0.5)
    k_ref[...] = norm_rope(proj(D_ATTN, 2 * D_ATTN), kn_ref, 1.0)
    v_ref[...] = proj(2 * D_ATTN, 3 * D_ATTN)

    o = 3 * D_ATTN
    zm = proj(o, o + 4 * D_MLSTM)
    lane = lax.broadcasted_iota(jnp.int32, (1, 4 * D_MLSTM), 1)
    zm_ref[...] = zm * jnp.where((lane >= D_MLSTM) & (lane < 2 * D_MLSTM), HEAD_DIM ** -0.5, 1.0)

    o += 4 * D_MLSTM
    gt = proj(o, o + GATE_PAD) + gb_ref[...]
    glane = lax.broadcasted_iota(jnp.int32, (1, GATE_PAD), 1)
    log_sig = jnp.minimum(gt, 0.0) - jnp.log1p(jnp.exp(-jnp.abs(gt)))
    gt_ref[...] = jnp.where(glane < N_HEADS_MLSTM, gt, log_sig)

    o += GATE_PAD
    cv = proj(o, o + D_CONV)
    cg = proj(o + D_CONV, o + 2 * D_CONV)
    u_ref[...] = cv * jax.nn.sigmoid(cg)


def _inproj(x, g, w_p, qn_row, kn_row, gb_row, bd, rc, rs1, rs2):
    T = x.shape[0]
    tm = _tile(T, 640, LANES)
    row = lambda i: (i, 0)
    const = lambda i: (0, 0)
    out_shapes = [jax.ShapeDtypeStruct((T, n), F32)
                  for n in (D_ATTN, D_ATTN, D_ATTN, 4 * D_MLSTM, GATE_PAD, D_CONV)]
    return pl.pallas_call(
        _inproj_kernel,
        grid=(T // tm,),
        in_specs=[pl.BlockSpec((tm, D_MODEL), row),
                  pl.BlockSpec((1, D_MODEL), const),
                  pl.BlockSpec((D_MODEL, W_IN_COLS), const),
                  pl.BlockSpec((1, D_ATTN), const),
                  pl.BlockSpec((1, D_ATTN), const),
                  pl.BlockSpec((1, GATE_PAD), const),
                  pl.BlockSpec((D_ATTN, D_ATTN), const),
                  pl.BlockSpec((tm, LANES), row),
                  pl.BlockSpec((tm, LANES), row),
                  pl.BlockSpec((tm, LANES), row)],
        out_specs=[pl.BlockSpec((tm, s.shape[1]), row) for s in out_shapes],
        out_shape=out_shapes,
        compiler_params=_params(("parallel",)),
        name="inproj",
    )(x, g, w_p, qn_row, kn_row, gb_row, bd, rc, rs1, rs2)


def _attn_prompt_kernel(q_ref, k_ref, v_ref, buf_ref, o_ref, qs, ks, vs, racc, rm, rden, nacc, nm, nden,
                        *, seq):
    del buf_ref
    blk = ATTN_BLOCK
    n_blocks = seq // blk
    lane = lax.broadcasted_iota(jnp.int32, (1, LANES), 1)
    head_a = lane < HEAD_DIM
    row = lax.broadcasted_iota(jnp.int32, (blk, 2 * blk), 0)
    col = lax.broadcasted_iota(jnp.int32, (blk, 2 * blk), 1)
    band = (col >= row) & (col <= row + blk)
    bias_band = jnp.where(band, 0.0, NEG)
    bias_first = jnp.where(band & (col >= blk), 0.0, NEG)

    ks[0:blk, :] = jnp.zeros((blk, LANES), BF16)
    vs[0:blk, :] = jnp.zeros((blk, LANES), BF16)

    cp = 256
    for (_, dil) in DILATED_BRANCHES:
        nsub = seq // dil
        nb = nsub // blk
        for r in range(dil):
            for c0 in range(0, nsub, cp):
                n = min(cp, nsub - c0)
                if dil == 1:
                    src = pl.ds(c0, n)
                else:
                    src = pl.ds(r + c0 * dil, n, stride=dil)
                dst = r * nsub + c0
                qs[dst:dst + n, :] = q_ref[src, :].astype(BF16)
                ks[blk + dst:blk + dst + n, :] = k_ref[src, :].astype(BF16)
                vs[blk + dst:blk + dst + n, :] = v_ref[src, :].astype(BF16)

        if dil == 1:
            dst_acc, dst_m, dst_den = nacc, nm, nden
        else:
            dst_acc, dst_m, dst_den = racc, rm, rden

        def body(j, carry, nb=nb, dst_acc=dst_acc, dst_m=dst_m, dst_den=dst_den):
            off = pl.multiple_of(j * blk, blk)
            qb = qs[pl.ds(off, blk), :]
            zero = jnp.zeros_like(qb)
            q2 = jnp.concatenate([jnp.where(head_a, qb, zero), jnp.where(head_a, zero, qb)], axis=0)
            kk = ks[pl.ds(off, 2 * blk), :]
            vv = vs[pl.ds(off, 2 * blk), :]
            s = lax.dot_general(q2, kk, (((1,), (1,)), ((), ())), preferred_element_type=F32)
            bias = jnp.where(j % nb == 0, bias_first, bias_band)
            s = s + jnp.concatenate([bias, bias], axis=0)
            mx = jnp.max(s, axis=1, keepdims=True)
            p = jnp.exp(s - mx)
            den = jnp.sum(p, axis=1, keepdims=True)
            o = jnp.dot(p.astype(BF16), vv, preferred_element_type=F32)
            dst_acc[pl.ds(off, blk), :] = jnp.where(head_a, o[:blk], o[blk:])
            dst_m[pl.ds(off, blk), :] = jnp.where(head_a, mx[:blk], mx[blk:])
            dst_den[pl.ds(off, blk), :] = jnp.where(head_a, den[:blk], den[blk:])
            return carry

        lax.fori_loop(0, n_blocks, body, 0, unroll=8)

        if dil > 1:
            for r in range(dil):
                for c0 in range(0, nsub, cp):
                    n = min(cp, nsub - c0)
                    nat = pl.ds(r + c0 * dil, n, stride=dil)
                    res = pl.ds(r * nsub + c0, n)
                    m_old, m_new = nm[nat, :], rm[res, :]
                    m_all = jnp.maximum(m_old, m_new)
                    e_old = jnp.exp(m_old - m_all)
                    e_new = jnp.exp(m_new - m_all)
                    nacc[nat, :] = e_old * nacc[nat, :] + e_new * racc[res, :]
                    nden[nat, :] = e_old * nden[nat, :] + e_new * rden[res, :]
                    nm[nat, :] = m_all

    o_ref[...] = nacc[...] / nden[...]


def _attn_prompt(q, k, v, buf, batch, seq):
    n_pairs = D_ATTN // LANES
    spec = pl.BlockSpec((seq, LANES), lambda b, hp: (b, hp))
    return pl.pallas_call(
        functools.partial(_attn_prompt_kernel, seq=seq),
        grid=(batch, n_pairs),
        in_specs=[spec, spec, spec, pl.BlockSpec(memory_space=pl.ANY)],
        out_specs=spec,
        out_shape=jax.ShapeDtypeStruct(buf.shape, F32),
        input_output_aliases={3: 0},
        scratch_shapes=[pltpu.VMEM((seq, LANES), BF16),
                        pltpu.VMEM((seq + ATTN_BLOCK, LANES), BF16),
                        pltpu.VMEM((seq + ATTN_BLOCK, LANES), BF16)]
                       + [pltpu.VMEM((seq, LANES), F32) for _ in range(6)],
        compiler_params=_params(("parallel", "parallel")),
        name="attn_prompt",
    )(q, k, v, buf)


def _attn_sample_kernel(q_ref, kn_ref, vn_ref, kc_ref, vc_ref, cw_ref, buf_ref, o_ref, kall, vall,
                        *, p_len, dec):
    del buf_ref
    pad_rows = kall.shape[0] - p_len
    kall[0:p_len, :] = kc_ref[...].astype(BF16)
    vall[0:p_len, :] = vc_ref[...].astype(BF16)
    tail = jnp.zeros((pad_rows - dec, D_ATTN), F32)
    kall[p_len:, :] = jnp.concatenate([kn_ref[...], tail], axis=0).astype(BF16)
    vall[p_len:, :] = jnp.concatenate([vn_ref[...], tail], axis=0).astype(BF16)

    lane = lax.broadcasted_iota(jnp.int32, (1, LANES), 1)
    head_a = lane < HEAD_DIM
    cw = cw_ref[...]
    cw2 = jnp.concatenate([cw, cw], axis=0)
    outs = []
    for hp in range(D_ATTN // LANES):
        sl = slice(hp * LANES, (hp + 1) * LANES)
        qb = q_ref[:, sl].astype(BF16)
        zero = jnp.zeros_like(qb)
        q2 = jnp.concatenate([jnp.where(head_a, qb, zero), jnp.where(head_a, zero, qb)], axis=0)
        s = lax.dot_general(q2, kall[:, sl], (((1,), (1,)), ((), ())), preferred_element_type=F32)
        s = jnp.where(cw2 > 0.0, s, NEG)
        mx = jnp.max(s, axis=1, keepdims=True)
        p = jnp.exp(s - mx) * cw2
        den = jnp.sum(p, axis=1, keepdims=True)
        o = jnp.dot(p.astype(BF16), vall[:, sl], preferred_element_type=F32) / den
        outs.append(jnp.where(head_a, o[:dec], o[dec:]))
    o_ref[...] = jnp.concatenate(outs, axis=1)


def _sample_multiplicity(p_len, dec, n_rows):
    t = np.arange(dec)[:, None]
    i = np.arange(n_rows)[None, :]
    dist = p_len + t - i
    cw = np.zeros((dec, n_rows), np.float32)
    for (w, d) in DILATED_BRANCHES:
        cw += ((dist >= 0) & (dist <= w) & (dist % d == 0) & (i < p_len + dec))
    return jnp.asarray(cw)


def _attn_sample(q, k, v, cache_k, cache_v, buf, layer, nb, dec, row0):
    p_len = cache_k.shape[2]
    n_rows = p_len + LANES
    cw = _sample_multiplicity(p_len, dec, n_rows)
    new = pl.BlockSpec((dec, D_ATTN), lambda b: (row0 // dec + b, 0))
    cache = pl.BlockSpec((None, None, p_len, D_ATTN), lambda b: (layer, b, 0, 0))
    return pl.pallas_call(
        functools.partial(_attn_sample_kernel, p_len=p_len, dec=dec),
        grid=(nb,),
        in_specs=[new, new, new, cache, cache, pl.BlockSpec((dec, n_rows), lambda b: (0, 0)),
                  pl.BlockSpec(memory_space=pl.ANY)],
        out_specs=new,
        out_shape=jax.ShapeDtypeStruct(buf.shape, F32),
        input_output_aliases={6: 0},
        scratch_shapes=[pltpu.VMEM((n_rows, D_ATTN), BF16), pltpu.VMEM((n_rows, D_ATTN), BF16)],
        compiler_params=_params(("parallel",)),
        name="attn_sample",
    )(q, k, v, cache_k, cache_v, cw, buf)


def _mlstm_kernel(zm_ref, gt_ref, c0_ref, nm0_ref, on_ref, buf_ref, h_ref, c1_ref, nm1_ref, c_s, nm_s,
                  *, rows):
    del buf_ref
    ch = MLSTM_CHUNK
    step = pl.program_id(1)

    @pl.when(step == 0)
    def _():
        c_s[...] = c0_ref[...]
        nm_s[...] = nm0_ref[...]

    def padded(a):
        if rows == ch:
            return a
        return jnp.concatenate([a, jnp.zeros((ch - rows, a.shape[1]), a.dtype)], axis=0)

    lane = lax.broadcasted_iota(jnp.int32, (1, LANES), 1)
    head_a = lane < HEAD_DIM
    ri = lax.broadcasted_iota(jnp.int32, (ch, ch), 0)
    ci = lax.broadcasted_iota(jnp.int32, (ch, ch), 1)
    causal = ci <= ri
    eye = ci == ri
    rowv = lax.broadcasted_iota(jnp.int32, (ch, 1), 0)
    valid = rowv < rows

    gt = padded(gt_ref[...])
    glane = lax.broadcasted_iota(jnp.int32, (1, LANES), 1)
    is_ig = glane < N_HEADS_MLSTM
    gt = jnp.where(valid, gt, jnp.where(is_ig, NEG, 0.0))
    lf_only = jnp.where(is_ig, 0.0, gt)
    fcum = jnp.dot(causal.astype(F32), lf_only, preferred_element_type=F32,
                   precision=lax.Precision.HIGHEST)

    nm = nm_s[...]
    m_row_new = nm[2:3, :]
    brow128 = lax.broadcasted_iota(jnp.int32, (LANES, 1), 0) < HEAD_DIM
    bd_mask = brow128 == head_a
    bd_pair = jnp.where(bd_mask, 1.0 / HEAD_DIM, 0.0).astype(BF16)
    ones_b = jnp.ones((ch, LANES), BF16)

    for pair in range(N_HEADS_MLSTM // 2):
        ls = slice(pair * LANES, (pair + 1) * LANES)
        qp = padded(zm_ref[:, ls])
        kp = padded(zm_ref[:, 2 * LANES + pair * LANES:2 * LANES + (pair + 1) * LANES])
        vp = padded(zm_ref[:, 4 * LANES + pair * LANES:4 * LANES + (pair + 1) * LANES])
        op = padded(zm_ref[:, 6 * LANES + pair * LANES:6 * LANES + (pair + 1) * LANES])
        kb, vb = kp.astype(BF16), vp.astype(BF16)
        c_pair = c_s[pair]
        cb = c_pair.astype(BF16)
        n_row = nm[pair:pair + 1, :]

        h_heads, wl_heads, decays = [], [], []
        for sub in range(2):
            hx = 2 * pair + sub
            mask = head_a if sub == 0 else jnp.logical_not(head_a)
            a = fcum[:, N_HEADS_MLSTM + hx:N_HEADS_MLSTM + hx + 1]
            igc = gt[:, hx:hx + 1]
            m0 = nm[2:3, hx:hx + 1]
            brow = jnp.sum(jnp.where(eye, igc - a, 0.0), axis=0, keepdims=True)
            bm = jnp.where(causal, brow, NEG)
            inter = a + m0
            m = jnp.maximum(inter, a + jnp.max(bm, axis=1, keepdims=True))
            a_rep = jnp.broadcast_to(a, (ch, LANES))
            m_rep = jnp.broadcast_to(m, (ch, LANES))
            ig_rep = jnp.broadcast_to(igc, (ch, LANES))
            am_rep = a_rep - m_rep
            w = jnp.exp(bm + am_rep)
            g_rep = jnp.exp(am_rep + m0)
            qx = jnp.where(mask, qp, 0.0)
            qxb = qx.astype(BF16)
            sc = lax.dot_general(qxb, kb, (((1,), (1,)), ((), ())), preferred_element_type=F32) * w
            num = (jnp.dot(sc.astype(BF16), vb, preferred_element_type=F32)
                   + g_rep * jnp.dot(qxb, cb, preferred_element_type=F32))
            den = _split_dot(sc, ones_b) + g_rep * _split_dot(qx * n_row, ones_b)
            h_heads.append(num / jnp.maximum(jnp.abs(den), jnp.exp(-m_rep)))
            m_last = m[rows - 1:rows, :]
            f_last = a[rows - 1:rows, :]
            decays.append(jnp.exp(f_last + m0 - m_last))
            wl_heads.append(jnp.exp((f_last - m_last) + ig_rep - a_rep))
            m_row_new = jnp.where(glane == hx, m_last, m_row_new)

        kw = kp * jnp.where(head_a, wl_heads[0], wl_heads[1])
        upd = lax.dot_general(kw.astype(BF16), vb, (((0,), (0,)), ((), ())), preferred_element_type=F32)
        c_s[pair] = jnp.where(bd_mask, jnp.where(brow128, decays[0], decays[1]) * c_pair + upd, 0.0)
        n_new = jnp.where(head_a, decays[0], decays[1]) * n_row + jnp.sum(kw, axis=0, keepdims=True)
        nm_s[pair:pair + 1, :] = n_new

        y = jax.nn.sigmoid(op) * jnp.where(head_a, h_heads[0], h_heads[1])
        ms = _split_dot(y * y, bd_pair)
        out = y * lax.rsqrt(ms + EPS) * on_ref[:, ls]
        h_ref[:, ls] = out[:rows]

    nm_s[2:3, :] = m_row_new

    @pl.when(step == pl.num_programs(1) - 1)
    def _():
        c1_ref[...] = c_s[...]
        nm1_ref[...] = nm_s[...]


def _mlstm(zm, gt, c0, nm0, on_row, buf, nb, seq, row0):
    rows = min(seq, MLSTM_CHUNK)
    n_chunks = seq // rows
    tok = lambda b, c: (row0 // rows + b * n_chunks + c, 0)
    state = lambda b, c: (b, 0, 0, 0)
    return pl.pallas_call(
        functools.partial(_mlstm_kernel, rows=rows),
        grid=(nb, n_chunks),
        in_specs=[pl.BlockSpec((rows, 4 * D_MLSTM), tok),
                  pl.BlockSpec((rows, GATE_PAD), tok),
                  pl.BlockSpec((None, 2, LANES, LANES), state),
                  pl.BlockSpec((None, 8, LANES), lambda b, c: (b, 0, 0)),
                  pl.BlockSpec((1, D_MLSTM), lambda b, c: (0, 0)),
                  pl.BlockSpec(memory_space=pl.ANY)],
        out_specs=[pl.BlockSpec((rows, D_MLSTM), tok),
                   pl.BlockSpec((None, 2, LANES, LANES), state),
                   pl.BlockSpec((None, 8, LANES), lambda b, c: (b, 0, 0))],
        out_shape=[jax.ShapeDtypeStruct(buf.shape, F32),
                   jax.ShapeDtypeStruct((nb, 2, LANES, LANES), F32),
                   jax.ShapeDtypeStruct((nb, 8, LANES), F32)],
        input_output_aliases={5: 0},
        scratch_shapes=[pltpu.VMEM((2, LANES, LANES), F32), pltpu.VMEM((8, LANES), F32)],
        compiler_params=_params(("parallel", "arbitrary")),
        name="mlstm",
    )(zm, gt, c0, nm0, on_row, buf)


def _conv_kernel(u_ref, hist_ref, w_ref, b_ref, g_ref, nb_ref, bd_ref, buf_ref, c_ref, st_ref, uc,
                 *, seq, rt):
    del buf_ref
    lead = HIST_PAD - (CONV_WIDTH - 1)
    uc[0:HIST_PAD, :] = hist_ref[...]
    uc[HIST_PAD:HIST_PAD + seq, :] = u_ref[...]
    uc[HIST_PAD + seq:, :] = jnp.zeros((SUBLANES, D_CONV), F32)
    w = w_ref[...]

    def chunk(c, carry):
        base = pl.multiple_of(c * rt, rt)
        win = uc[pl.ds(base, rt + HIST_PAD + SUBLANES), :]
        acc = None
        for b in range(SUBLANES):
            inner = None
            for a in range((lead + CONV_WIDTH - 1) // SUBLANES + 1):
                j = SUBLANES * a + b - lead
                if 0 <= j < CONV_WIDTH:
                    term = win[SUBLANES * a:SUBLANES * a + rt + SUBLANES, :] * w[j:j + 1, :]
                    inner = term if inner is None else inner + term
            part = inner[b:b + rt, :]
            acc = part if acc is None else acc + part
        y = acc + b_ref[...]
        mu = _group_mean(y, bd_ref)
        yc = y - mu
        var = _group_mean(yc * yc, bd_ref)
        z = yc * lax.rsqrt(var + EPS) * g_ref[...] + nb_ref[...]
        c_ref[pl.ds(base, rt), :] = z * jax.nn.sigmoid(z)
        return carry

    lax.fori_loop(0, seq // rt, chunk, 0)
    st_ref[...] = uc[seq:seq + HIST_PAD, :]


def _conv(u, hist, w, b, g, nbias, bd, buf, nb, seq, row0):
    rt = min(seq, 256)
    tok = lambda i: (row0 // seq + i, 0)
    const = lambda i: (0, 0)
    hspec = pl.BlockSpec((None, HIST_PAD, D_CONV), lambda i: (i, 0, 0))
    return pl.pallas_call(
        functools.partial(_conv_kernel, seq=seq, rt=rt),
        grid=(nb,),
        in_specs=[pl.BlockSpec((seq, D_CONV), tok), hspec,
                  pl.BlockSpec((HIST_PAD, D_CONV), const),
                  pl.BlockSpec((1, D_CONV), const), pl.BlockSpec((1, D_CONV), const),
                  pl.BlockSpec((1, D_CONV), const), pl.BlockSpec((D_CONV, D_CONV), const),
                  pl.BlockSpec(memory_space=pl.ANY)],
        out_specs=[pl.BlockSpec((seq, D_CONV), tok), hspec],
        out_shape=[jax.ShapeDtypeStruct(buf.shape, F32),
                   jax.ShapeDtypeStruct((nb, HIST_PAD, D_CONV), F32)],
        input_output_aliases={7: 0},
        scratch_shapes=[pltpu.VMEM((seq + HIST_PAD + SUBLANES, D_CONV), F32)],
        compiler_params=_params(("parallel",)),
        name="conv",
    )(u, hist, w, b, g, nbias, bd, buf)


def _outproj_kernel(x_ref, a_ref, m_ref, c_ref, w_ref, o_ref):
    acc = jnp.dot(a_ref[...].astype(BF16), w_ref[0:D_ATTN, :], preferred_element_type=F32)
    acc += jnp.dot(m_ref[...].astype(BF16), w_ref[D_ATTN:D_ATTN + D_MLSTM, :], preferred_element_type=F32)
    acc += jnp.dot(c_ref[...].astype(BF16), w_ref[D_ATTN + D_MLSTM:, :], preferred_element_type=F32)
    o_ref[...] = x_ref[...] + acc


def _outproj(x, att, mo, cv, w_out):
    T = x.shape[0]
    tm = _tile(T, 640, LANES)
    row = lambda i: (i, 0)
    return pl.pallas_call(
        _outproj_kernel,
        grid=(T // tm,),
        in_specs=[pl.BlockSpec((tm, D_MODEL), row), pl.BlockSpec((tm, D_ATTN), row),
                  pl.BlockSpec((tm, D_MLSTM), row), pl.BlockSpec((tm, D_CONV), row),
                  pl.BlockSpec((D_MODEL, D_MODEL), lambda i: (0, 0))],
        out_specs=pl.BlockSpec((tm, D_MODEL), row),
        out_shape=jax.ShapeDtypeStruct((T, D_MODEL), F32),
        compiler_params=_params(("parallel",)),
        name="outproj",
    )(x, att, mo, cv, w_out)


def _ffn_kernel(x_ref, g_ref, w1_ref, w3_ref, w2_ref, o_ref, h_s, acc_s):
    j = pl.program_id(1)

    @pl.when(j == 0)
    def _():
        x = x_ref[...]
        h_s[...] = (x * lax.rsqrt(jnp.mean(x * x, axis=-1, keepdims=True) + EPS) * g_ref[...]).astype(BF16)
        acc_s[...] = x

    h = h_s[...]
    a = jnp.dot(h, w1_ref[...], preferred_element_type=F32)
    g = jnp.dot(h, w3_ref[...], preferred_element_type=F32)
    act = (a * jax.nn.sigmoid(a) * g).astype(BF16)
    acc_s[...] += jnp.dot(act, w2_ref[...], preferred_element_type=F32)

    @pl.when(j == pl.num_programs(1) - 1)
    def _():
        o_ref[...] = acc_s[...]


def _ffn(x, g, w1, w3, w2):
    T = x.shape[0]
    tm = _tile(T, 640, LANES)
    tf = D_FF // 2
    row = lambda i, j: (i, 0)
    return pl.pallas_call(
        _ffn_kernel,
        grid=(T // tm, D_FF // tf),
        in_specs=[pl.BlockSpec((tm, D_MODEL), row), pl.BlockSpec((1, D_MODEL), lambda i, j: (0, 0)),
                  pl.BlockSpec((D_MODEL, tf), lambda i, j: (0, j)),
                  pl.BlockSpec((D_MODEL, tf), lambda i, j: (0, j)),
                  pl.BlockSpec((tf, D_MODEL), lambda i, j: (j, 0))],
        out_specs=pl.BlockSpec((tm, D_MODEL), row),
        out_shape=jax.ShapeDtypeStruct((T, D_MODEL), F32),
        scratch_shapes=[pltpu.VMEM((tm, D_MODEL), BF16), pltpu.VMEM((tm, D_MODEL), F32)],
        compiler_params=_params(("parallel", "arbitrary")),
        name="ffn",
    )(x, g, w1, w3, w2)


def _to_row_tiles(ref, val):
    n = val.shape[0]
    for c in range(ROW_CHUNKS):
        ref[pl.ds(c, n, stride=ROW_CHUNKS), :] = val[:, c * LANES:(c + 1) * LANES]


def _from_row_tiles(ref, n):
    return jnp.concatenate([ref[pl.ds(c, n, stride=ROW_CHUNKS), :] for c in range(ROW_CHUNKS)], axis=1)


def _router_kernel(x_ref, g_ref, r_ref, comb_ref, xt_ref):
    x = x_ref[...]
    _to_row_tiles(xt_ref, x)
    h = x * lax.rsqrt(jnp.mean(x * x, axis=-1, keepdims=True) + EPS) * g_ref[...]
    logits = jnp.dot(h, r_ref[...], preferred_element_type=F32, precision=lax.Precision.HIGHEST)
    lane = lax.broadcasted_iota(jnp.int32, logits.shape, 1)
    logits = jnp.where(lane < N_EXPERTS, logits, NEG)
    v1 = jnp.max(logits, axis=1, keepdims=True)
    i1 = jnp.min(jnp.where(logits == v1, lane, LANES), axis=1, keepdims=True)
    rest = jnp.where(lane == i1, NEG, logits)
    v2 = jnp.max(rest, axis=1, keepdims=True)
    i2 = jnp.min(jnp.where(rest == v2, lane, LANES), axis=1, keepdims=True)
    e2 = jnp.exp(v2 - v1)
    inv = 1.0 / (1.0 + e2)
    route = jnp.where(lane == ROUTE_I1, i1.astype(F32), jnp.where(lane == ROUTE_I2, i2.astype(F32), 0.0))
    comb_ref[...] = jnp.where(lane == ROUTE_G1, inv, jnp.where(lane == ROUTE_G2, e2 * inv, route))


def _router(x, g, r_pad):
    T = x.shape[0]
    tm = _tile(T, 640, LANES)
    row = lambda i: (i, 0)
    return pl.pallas_call(
        _router_kernel,
        grid=(T // tm,),
        in_specs=[pl.BlockSpec((tm, D_MODEL), row), pl.BlockSpec((1, D_MODEL), lambda i: (0, 0)),
                  pl.BlockSpec((D_MODEL, LANES), lambda i: (0, 0))],
        out_specs=[pl.BlockSpec((tm, LANES), row), pl.BlockSpec((tm * ROW_CHUNKS, LANES), row)],
        out_shape=[jax.ShapeDtypeStruct((T, LANES), F32), jax.ShapeDtypeStruct((T * ROW_CHUNKS, LANES), F32)],
        compiler_params=_params(("parallel",)),
        name="router",
    )(x, g, r_pad)


def _route_plan(i1, i2, tg, n_tiles):
    e = jnp.stack([i1, i2], axis=1).reshape(-1)
    oh = (e[:, None] == jnp.arange(N_EXPERTS, dtype=jnp.int32)[None, :]).astype(jnp.int32)
    csum = jnp.cumsum(oh, axis=0)
    counts = csum[-1]
    tiles_per = (counts + tg - 1) // tg
    tile_end = jnp.cumsum(tiles_per)
    tile_start = tile_end - tiles_per
    pos = (jnp.sum(oh * (csum + (tile_start * tg)[None, :]), axis=1) - 1).astype(jnp.int32)
    tile_ids = jnp.arange(n_tiles, dtype=jnp.int32)
    tile_expert = jnp.minimum(jnp.sum((tile_ids[:, None] >= tile_end[None, :]).astype(jnp.int32), axis=1),
                              N_EXPERTS - 1).astype(jnp.int32)
    tile_valid = (tile_ids < tile_end[-1]).astype(jnp.int32)
    return pos, tile_expert, tile_valid


def _moe_scatter_kernel(pos_ref, xt_ref, init_hbm, xs_hbm, sem, *, tm):
    del init_hbm
    base = pl.program_id(0) * tm

    def issue(r, carry):
        src = xt_ref.at[pl.ds(pl.multiple_of(r * ROW_CHUNKS, ROW_CHUNKS), ROW_CHUNKS), :]
        for choice in range(2):
            p = pos_ref[2 * (base + r) + choice]
            dst = xs_hbm.at[pl.ds(pl.multiple_of(p * ROW_CHUNKS, ROW_CHUNKS), ROW_CHUNKS), :]
            pltpu.make_async_copy(src, dst, sem.at[choice]).start(priority=choice)
        return carry

    lax.fori_loop(0, tm, issue, 0, unroll=4)
    for choice in range(2):
        pltpu.make_async_copy(xt_ref, xs_hbm.at[pl.ds(0, tm * ROW_CHUNKS), :], sem.at[choice]).wait()


def _moe_scatter(xt, pos, n_rows):
    T = xt.shape[0] // ROW_CHUNKS
    tm = _tile(T, 640, LANES)
    init = jnp.zeros((n_rows * ROW_CHUNKS, LANES), F32)
    return pl.pallas_call(
        functools.partial(_moe_scatter_kernel, tm=tm),
        grid_spec=pltpu.PrefetchScalarGridSpec(
            num_scalar_prefetch=1,
            grid=(T // tm,),
            in_specs=[pl.BlockSpec((tm * ROW_CHUNKS, LANES), lambda i, pos: (i, 0)),
                      pl.BlockSpec(memory_space=pl.ANY)],
            out_specs=pl.BlockSpec(memory_space=pl.ANY),
            scratch_shapes=[pltpu.SemaphoreType.DMA((2,))]),
        out_shape=jax.ShapeDtypeStruct(init.shape, F32),
        input_output_aliases={2: 0},
        compiler_params=_params(("arbitrary",)),
        name="moe_scatter",
    )(pos, xt, init)


def _moe_group_kernel(te_ref, tv_ref, xs_ref, g_ref, w1_ref, w3_ref, w2_ref, o_ref, *, tg):
    del te_ref
    i = pl.program_id(0)

    @pl.when(tv_ref[i] > 0)
    def _():
        x = _from_row_tiles(xs_ref, tg)
        h = (x * lax.rsqrt(jnp.mean(x * x, axis=-1, keepdims=True) + EPS) * g_ref[...]).astype(BF16)
        acc = jnp.zeros((tg, D_MODEL), F32)
        for c in range(0, D_FF, MOE_FF_CHUNK):
            a = jnp.dot(h, w1_ref[:, c:c + MOE_FF_CHUNK], preferred_element_type=F32)
            g = jnp.dot(h, w3_ref[:, c:c + MOE_FF_CHUNK], preferred_element_type=F32)
            act = (a * jax.nn.sigmoid(a) * g).astype(BF16)
            acc = acc + jnp.dot(act, w2_ref[c:c + MOE_FF_CHUNK, :], preferred_element_type=F32)
        _to_row_tiles(o_ref, acc)

    @pl.when(tv_ref[i] == 0)
    def _():
        o_ref[...] = jnp.zeros((tg * ROW_CHUNKS, LANES), F32)


def _moe_group(xs, g, w1, w3, w2, tile_expert, tile_valid, tg, n_tiles):
    wspec = lambda shape: pl.BlockSpec((None,) + shape, lambda i, te, tv: (te[i], 0, 0))
    rows = pl.BlockSpec((tg * ROW_CHUNKS, LANES), lambda i, te, tv: (i, 0))
    return pl.pallas_call(
        functools.partial(_moe_group_kernel, tg=tg),
        grid_spec=pltpu.PrefetchScalarGridSpec(
            num_scalar_prefetch=2,
            grid=(n_tiles,),
            in_specs=[rows, pl.BlockSpec((1, D_MODEL), lambda i, te, tv: (0, 0)),
                      wspec((D_MODEL, D_FF)), wspec((D_MODEL, D_FF)), wspec((D_FF, D_MODEL))],
            out_specs=rows),
        out_shape=jax.ShapeDtypeStruct((n_tiles * tg * ROW_CHUNKS, LANES), F32),
        compiler_params=_params(("arbitrary",)),
        name="moe_group",
    )(tile_expert, tile_valid, xs, g, w1, w3, w2)


def _moe_combine_kernel(pos_ref, x_ref, route_ref, ys_hbm, o_ref, yb, sem, *, tm, n_steps):
    i = pl.program_id(0)

    def gather(step, slot):
        base = step * (2 * tm)

        def issue(r, carry):
            for choice in range(2):
                p = pos_ref[base + 2 * r + choice]
                src = ys_hbm.at[pl.ds(pl.multiple_of(p * ROW_CHUNKS, ROW_CHUNKS), ROW_CHUNKS), :]
                dst = yb.at[slot, choice, pl.ds(pl.multiple_of(r * ROW_CHUNKS, ROW_CHUNKS), ROW_CHUNKS), :]
                pltpu.make_async_copy(src, dst, sem.at[slot]).start(priority=choice)
            return carry

        lax.fori_loop(0, tm, issue, 0, unroll=4)

    @pl.when(i == 0)
    def _():
        gather(0, 0)

    @pl.when(i + 1 < n_steps)
    def _():
        gather(i + 1, (i + 1) % 2)

    slot = i % 2
    pltpu.make_async_copy(yb.at[slot], yb.at[slot], sem.at[slot]).wait()
    route = route_ref[...]
    g1 = route[:, ROUTE_G1:ROUTE_G1 + 1]
    g2 = route[:, ROUTE_G2:ROUTE_G2 + 1]
    o_ref[...] = x_ref[...] + g1 * _from_row_tiles(yb.at[slot, 0], tm) + g2 * _from_row_tiles(yb.at[slot, 1], tm)


def _moe_combine(x, route, ys, pos):
    T = x.shape[0]
    tm = _tile(T, 640, LANES)
    n_steps = T // tm
    row = lambda i, pos: (i, 0)
    return pl.pallas_call(
        functools.partial(_moe_combine_kernel, tm=tm, n_steps=n_steps),
        grid_spec=pltpu.PrefetchScalarGridSpec(
            num_scalar_prefetch=1,
            grid=(n_steps,),
            in_specs=[pl.BlockSpec((tm, D_MODEL), row), pl.BlockSpec((tm, LANES), row),
                      pl.BlockSpec(memory_space=pl.ANY)],
            out_specs=pl.BlockSpec((tm, D_MODEL), row),
            scratch_shapes=[pltpu.VMEM((2, 2, tm * ROW_CHUNKS, LANES), F32), pltpu.SemaphoreType.DMA((2,))]),
        out_shape=jax.ShapeDtypeStruct((T, D_MODEL), F32),
        compiler_params=_params(("arbitrary",)),
        name="moe_combine",
    )(pos, x, route, ys)


def _moe(x, g, r_pad, w1, w3, w2):
    T = x.shape[0]
    route, xt = _router(x, g, r_pad)
    i1 = route[:, ROUTE_I1].astype(jnp.int32)
    i2 = route[:, ROUTE_I2].astype(jnp.int32)
    tg = MOE_TILE
    n_tiles = pl.cdiv(2 * T, tg) + N_EXPERTS
    pos, tile_expert, tile_valid = _route_plan(i1, i2, tg, n_tiles)
    xs = _moe_scatter(xt, pos, n_tiles * tg)
    ys = _moe_group(xs, g, w1, w3, w2, tile_expert, tile_valid, tg, n_tiles)
    return _moe_combine(x, route, ys, pos)


def _rope_tables(pos):
    half = ROPE_DIM // 2
    inv_freq = np.float32(ROPE_THETA) ** (np.float32(-2.0) * np.arange(half, dtype=np.float32) / np.float32(ROPE_DIM))
    ang = (pos.astype(np.float32)[:, None] * inv_freq[None, :]).astype(np.float64)
    cos, sin = np.cos(ang).astype(np.float32), np.sin(ang).astype(np.float32)
    n = pos.shape[0]
    ones = np.ones((n, HEAD_DIM - ROPE_DIM), np.float32)
    zeros = np.zeros((n, HEAD_DIM - ROPE_DIM), np.float32)
    zh = np.zeros((n, half), np.float32)
    c = np.concatenate([cos, cos, ones], axis=1)
    s1 = np.concatenate([zh, sin, zeros], axis=1)
    s2 = np.concatenate([-sin, zh, zeros], axis=1)
    two = lambda t: jnp.asarray(np.concatenate([t, t], axis=1))
    return two(c), two(s1), two(s2)


def _kv_out_kernel(*refs, depth):
    k_refs, v_refs = refs[:depth], refs[depth:2 * depth]
    ko_ref, vo_ref = refs[2 * depth:]
    layer = pl.program_id(0)
    for d in range(depth):
        @pl.when(layer == d)
        def _(d=d):
            kx, vx = k_refs[d][...], v_refs[d][...]
            n = kx.shape[0]
            for h in range(SUBLANES):
                dst = pl.ds(h, n, stride=SUBLANES)
                if h < N_HEADS_ATTN:
                    ko_ref[dst, :] = kx[:, h * HEAD_DIM:(h + 1) * HEAD_DIM]
                    vo_ref[dst, :] = vx[:, h * HEAD_DIM:(h + 1) * HEAD_DIM]
                else:
                    ko_ref[dst, :] = jnp.zeros((n, HEAD_DIM), F32)
                    vo_ref[dst, :] = jnp.zeros((n, HEAD_DIM), F32)


def _kv_out(ks, vs, batch, seq, keep):
    depth = len(ks)
    rows = _tile(keep, 512, SUBLANES)
    first = seq - keep
    src = pl.BlockSpec((rows, D_ATTN), lambda l, b, j: ((b * seq + first) // rows + j, 0))
    dst = pl.BlockSpec((None, None, rows * SUBLANES, HEAD_DIM), lambda l, b, j: (l, b, j, 0))
    shape = jax.ShapeDtypeStruct((depth, batch, keep * SUBLANES, HEAD_DIM), F32)
    k8, v8 = pl.pallas_call(
        functools.partial(_kv_out_kernel, depth=depth),
        grid=(depth, batch, keep // rows),
        in_specs=[src] * (2 * depth),
        out_specs=[dst, dst],
        out_shape=[shape, shape],
        compiler_params=_params(("parallel", "parallel", "parallel")),
        name="kv_out",
    )(*ks, *vs)
    heads = lambda t: t.reshape(depth, batch, keep, SUBLANES, HEAD_DIM)[:, :, :, :N_HEADS_ATTN, :]
    return heads(k8), heads(v8)


def _pair_state(c, n, m):
    nb = c.shape[0]
    c = c.reshape(nb, 2, 2, HEAD_DIM, HEAD_DIM)
    z = jnp.zeros((nb, 2, HEAD_DIM, HEAD_DIM), F32)
    top = jnp.concatenate([c[:, :, 0], z], axis=-1)
    bot = jnp.concatenate([z, c[:, :, 1]], axis=-1)
    cp = jnp.concatenate([top, bot], axis=-2)
    nm = jnp.zeros((nb, 8, LANES), F32)
    nm = nm.at[:, 0:2, :].set(n.reshape(nb, 2, LANES))
    nm = nm.at[:, 2, 0:N_HEADS_MLSTM].set(m)
    return cp, nm


def _unpair_state(cp, nm):
    nb = cp.shape[0]
    c = jnp.stack([cp[:, :, :HEAD_DIM, :HEAD_DIM], cp[:, :, HEAD_DIM:, HEAD_DIM:]], axis=2)
    c = c.reshape(nb, N_HEADS_MLSTM, HEAD_DIM, HEAD_DIM)
    n = nm[:, 0:2, :].reshape(nb, N_HEADS_MLSTM, HEAD_DIM)
    m = nm[:, 2, 0:N_HEADS_MLSTM]
    return c, n, m


def kernel(x_prompt, x_sample, cache_attn_k, cache_attn_v, state_mlstm_C, state_mlstm_n, state_mlstm_m, state_conv, norm_mix, w_in, q_norm, k_norm, mlstm_gate_bias, mlstm_out_norm, conv_dw_w, conv_dw_b, conv_norm_g, conv_norm_b, w_out, norm_ffn, ffn_w1, ffn_w3, ffn_w2, moe_router, moe_w1, moe_w3, moe_w2):
    batch, seq, _ = x_prompt.shape
    nbs, dec, _ = x_sample.shape
    depth = w_in.shape[0]
    tp = batch * seq
    T = tp + nbs * dec
    keep = min(max(w for w, _ in DILATED_BRANCHES), seq)
    hist_rows = CONV_WIDTH - 1

    x = jnp.concatenate([x_prompt.reshape(tp, D_MODEL), x_sample.reshape(nbs * dec, D_MODEL)], axis=0)

    pos = np.concatenate([np.tile(np.arange(seq, dtype=np.float32), batch),
                          np.tile(np.float32(PAST_LEN) + np.arange(dec, dtype=np.float32), nbs)])
    rc, rs1, rs2 = _rope_tables(pos)
    gi = np.arange(D_ATTN) // HEAD_DIM
    bd = jnp.asarray((gi[:, None] == gi[None, :]).astype(np.float32) / HEAD_DIM, dtype=BF16)
    p_len = cache_attn_k.shape[2]
    cache_k = cache_attn_k.reshape(depth, nbs, p_len, D_ATTN)
    cache_v = cache_attn_v.reshape(depth, nbs, p_len, D_ATTN)

    zero_c = jnp.zeros((batch, 2, LANES, LANES), F32)
    zero_nm = jnp.zeros((batch, 8, LANES), F32)
    zero_hist = jnp.zeros((batch, HIST_PAD, D_CONV), F32)

    outs = [[] for _ in range(10)]
    ks, vs = [], []
    for l in range(depth):
        o = 3 * D_ATTN + 4 * D_MLSTM
        w_p = jnp.concatenate([w_in[l][:, :o], jnp.pad(w_in[l][:, o:o + 8], ((0, 0), (0, GATE_PAD - 8))),
                               w_in[l][:, o + 8:]], axis=1).astype(BF16)
        qn_row = jnp.tile(q_norm[l], N_HEADS_ATTN)[None, :]
        kn_row = jnp.tile(k_norm[l], N_HEADS_ATTN)[None, :]
        gb_row = jnp.pad(mlstm_gate_bias[l], (0, GATE_PAD - 2 * N_HEADS_MLSTM))[None, :]

        q, k, v, zm, gt, u = _inproj(x, norm_mix[l][None, :], w_p, qn_row, kn_row, gb_row, bd, rc, rs1, rs2)

        on_row = mlstm_out_norm[l][None, :]
        w_conv = jnp.pad(conv_dw_w[l], ((0, HIST_PAD - CONV_WIDTH), (0, 0)))
        conv_args = (w_conv, conv_dw_b[l][None, :], conv_norm_g[l][None, :], conv_norm_b[l][None, :], bd)

        att = _attn_sample(q, k, v, cache_k, cache_v, jnp.zeros((T, D_ATTN), F32), l, nbs, dec, tp)
        c0_s, nm0_s = _pair_state(state_mlstm_C[l], state_mlstm_n[l], state_mlstm_m[l])
        hm, c1_s, nm1_s = _mlstm(zm, gt, c0_s, nm0_s, on_row, jnp.zeros((T, D_MLSTM), F32), nbs, dec, tp)
        hist_s = jnp.pad(state_conv[l], ((0, 0), (HIST_PAD - hist_rows, 0), (0, 0)))
        cv, st_s = _conv(u, hist_s, *conv_args, jnp.zeros((T, D_CONV), F32), nbs, dec, tp)

        att = _attn_prompt(q, k, v, att, batch, seq)
        hm, c1_p, nm1_p = _mlstm(zm, gt, zero_c, zero_nm, on_row, hm, batch, seq, 0)
        cv, st_p = _conv(u, zero_hist, *conv_args, cv, batch, seq, 0)

        x = _outproj(x, att, hm, cv, w_out[l].astype(BF16))

        i = l // 2
        if l % 2 == 0:
            x = _ffn(x, norm_ffn[l][None, :], ffn_w1[i].astype(BF16), ffn_w3[i].astype(BF16),
                     ffn_w2[i].astype(BF16))
        else:
            r_pad = jnp.pad(moe_router[i], ((0, 0), (0, LANES - N_EXPERTS)))
            x = _moe(x, norm_ffn[l][None, :], r_pad, moe_w1[i].astype(BF16), moe_w3[i].astype(BF16),
                     moe_w2[i].astype(BF16))

        ks.append(k)
        vs.append(v)
        cp_, np_, mp_ = _unpair_state(c1_p, nm1_p)
        cs_, ns_, ms_ = _unpair_state(c1_s, nm1_s)
        layer_outs = (k[tp:].reshape(nbs, dec, N_HEADS_ATTN, HEAD_DIM),
                      v[tp:].reshape(nbs, dec, N_HEADS_ATTN, HEAD_DIM),
                      cp_, np_, mp_, cs_, ns_, ms_,
                      st_p[:, HIST_PAD - hist_rows:], st_s[:, HIST_PAD - hist_rows:])
        for lst, val in zip(outs, layer_outs):
            lst.append(val)

    k_prompt, v_prompt = _kv_out(ks, vs, batch, seq, keep)
    y_prompt = x[:tp].reshape(batch, seq, D_MODEL)
    y_sample = x[tp:].reshape(nbs, dec, D_MODEL)
    return (y_prompt, y_sample, k_prompt, v_prompt) + tuple(jnp.stack(lst, axis=0) for lst in outs)
```

```python
import functools

import numpy as np
import jax
import jax.numpy as jnp
from jax import lax
from jax.experimental import pallas as pl
from jax.experimental.pallas import tpu as pltpu

F32 = jnp.float32
BF16 = jnp.bfloat16

D_MODEL = 1024
HEAD_DIM = 64
N_HEADS_ATTN = 6
N_HEADS_MLSTM = 4
D_ATTN = N_HEADS_ATTN * HEAD_DIM
D_MLSTM = N_HEADS_MLSTM * HEAD_DIM
D_CONV = 384
DILATED_BRANCHES = ((128, 1), (512, 4), (2048, 16))
ATTN_BLOCK = 128
ROPE_DIM = HEAD_DIM // 4
ROPE_THETA = 500000.0
MLSTM_CHUNK = 128
CONV_WIDTH = 31
D_FF = 2816
N_EXPERTS = 8
EPS = 1e-6
NEG = -1e30
PAST_LEN = 16384

LANES = 128
SUBLANES = 8
ROW_CHUNKS = D_MODEL // LANES
HIST_PAD = 32
GATE_PAD = LANES
W_IN_COLS = 3 * D_ATTN + 4 * D_MLSTM + GATE_PAD + 2 * D_CONV
VMEM_LIMIT = 56 * 1024 * 1024
CONV_SUB_ROWS = 64
MLSTM_GROUP = 4
ATTN_INTERLEAVE = 4
MOE_TILE = 512
MOE_FF_CHUNK = 256
ROUTE_I1, ROUTE_I2, ROUTE_G1, ROUTE_G2 = 8, 9, 10, 11


def _tile(n, pref, mult=8):
    for t in range(min(pref, n), 0, -1):
        if n % t == 0 and t % mult == 0:
            return t
    return n


def _params(sem, flags=None):
    return pltpu.CompilerParams(dimension_semantics=sem, vmem_limit_bytes=VMEM_LIMIT, flags=flags)


def _round_robin(chains):
    while chains:
        chains = [c for c in chains if next(c, True) is None]
        yield


def _split_dot(a, b_bf16):
    hi = a.astype(BF16)
    lo = (a - hi.astype(F32)).astype(BF16)
    return (jnp.dot(hi, b_bf16, preferred_element_type=F32)
            + jnp.dot(lo, b_bf16, preferred_element_type=F32))


def _group_mean(a, bd_ref):
    return _split_dot(a, bd_ref[...])


def _group_specs(width, tb, per_tile, n_prompt):
    prompt = [pl.BlockSpec((tb, width), lambda i, k=k: (jnp.minimum(i * per_tile + k, n_prompt - 1), 0))
              for k in range(per_tile)]
    sample = [pl.BlockSpec((tb, width), lambda i, k=k: (jnp.maximum(i * per_tile + k - n_prompt, 0), 0))
              for k in range(per_tile)]
    return prompt + sample


def _group_rows(refs, per_tile, n_prompt):
    i = pl.program_id(0)
    return jnp.concatenate([jnp.where(i * per_tile + k < n_prompt, refs[k][...], refs[per_tile + k][...])
                            for k in range(per_tile)], axis=0)


def _inproj_kernel(*refs, split):
    n_x = 2 * split[0] if split else 1
    (g_ref, w_ref, qn_ref, kn_ref, gb_ref, bd_ref, rc_ref, rs1_ref, rs2_ref,
     q_ref, k_ref, v_ref, zm_ref, gt_ref, u_ref) = refs[n_x:]
    x = _group_rows(refs[:n_x], *split) if split else refs[0][...]
    h = (x * lax.rsqrt(jnp.mean(x * x, axis=-1, keepdims=True) + EPS) * g_ref[...]).astype(BF16)

    def proj(lo, hi):
        return jnp.dot(h, w_ref[:, lo:hi], preferred_element_type=F32)

    rc, rs1, rs2 = rc_ref[...], rs1_ref[...], rs2_ref[...]

    def norm_rope(z, gain_ref, scale):
        zn = z * lax.rsqrt(_group_mean(z * z, bd_ref) + EPS) * gain_ref[...]
        outs = []
        for c in range(D_ATTN // LANES):
            zc = zn[:, c * LANES:(c + 1) * LANES]
            y = (zc * rc + pltpu.roll(zc, ROPE_DIM // 2, 1) * rs1
                 + pltpu.roll(zc, LANES - ROPE_DIM // 2, 1) * rs2)
            outs.append(y * scale if scale != 1.0 else y)
        return jnp.concatenate(outs, axis=1)

    q_ref[...] = norm_rope(proj(0, D_ATTN), qn_ref, HEAD_DIM ** -0.5)
    k_ref[...] = norm_rope(proj(D_ATTN, 2 * D_ATTN), kn_ref, 1.0)
    v_ref[...] = proj(2 * D_ATTN, 3 * D_ATTN)

    o = 3 * D_ATTN
    zm = proj(o, o + 4 * D_MLSTM)
    lane = lax.broadcasted_iota(jnp.int32, (1, 4 * D_MLSTM), 1)
    zm_ref[...] = zm * jnp.where((lane >= D_MLSTM) & (lane < 2 * D_MLSTM), HEAD_DIM ** -0.5, 1.0)

    o += 4 * D_MLSTM
    gt = proj(o, o + GATE_PAD) + gb_ref[...]
    glane = lax.broadcasted_iota(jnp.int32, (1, GATE_PAD), 1)
    log_sig = jnp.minimum(gt, 0.0) - jnp.log1p(jnp.exp(-jnp.abs(gt)))
    gt_ref[...] = jnp.where(glane < N_HEADS_MLSTM, gt, log_sig)

    o += GATE_PAD
    cv = proj(o, o + D_CONV)
    cg = proj(o + D_CONV, o + 2 * D_CONV)
    u_ref[...] = cv * jax.nn.sigmoid(cg)


def _inproj(x, g, w_p, qn_row, kn_row, gb_row, bd, rc, rs1, rs2):
    row = lambda i: (i, 0)
    const = lambda i: (0, 0)
    if isinstance(x, tuple):
        tp, ts = x[0].shape[0], x[1].shape[0]
        T = tp + ts
        tm = _tile(T, 640, LANES)
        tb = _tile(np.gcd(np.gcd(tp, ts), tm), tm, SUBLANES)
        split = (tm // tb, tp // tb)
        x_specs = _group_specs(D_MODEL, tb, *split)
        x_args = [x[0]] * split[0] + [x[1]] * split[0]
    else:
        T = x.shape[0]
        tm = _tile(T, 640, LANES)
        split, x_specs, x_args = None, [pl.BlockSpec((tm, D_MODEL), row)], [x]
    out_shapes = [jax.ShapeDtypeStruct((T, n), F32)
                  for n in (D_ATTN, D_ATTN, D_ATTN, 4 * D_MLSTM, GATE_PAD, D_CONV)]
    return pl.pallas_call(
        functools.partial(_inproj_kernel, split=split),
        grid=(T // tm,),
        in_specs=x_specs + [
                  pl.BlockSpec((1, D_MODEL), const),
                  pl.BlockSpec((D_MODEL, W_IN_COLS), const),
                  pl.BlockSpec((1, D_ATTN), const),
                  pl.BlockSpec((1, D_ATTN), const),
                  pl.BlockSpec((1, GATE_PAD), const),
                  pl.BlockSpec((D_ATTN, D_ATTN), const),
                  pl.BlockSpec((tm, LANES), row),
                  pl.BlockSpec((tm, LANES), row),
                  pl.BlockSpec((tm, LANES), row)],
        out_specs=[pl.BlockSpec((tm, s.shape[1]), row) for s in out_shapes],
        out_shape=out_shapes,
        compiler_params=_params(("parallel",)),
        name="inproj",
    )(*x_args, g, w_p, qn_row, kn_row, gb_row, bd, rc, rs1, rs2)


def _attn_prompt_kernel(q_ref, k_ref, v_ref, buf_ref, o_ref, qs, ks, vs, racc, rm, rden, nacc, nm, nden,
                        *, seq):
    del buf_ref
    blk = ATTN_BLOCK
    n_blocks = seq // blk
    lane = lax.broadcasted_iota(jnp.int32, (1, LANES), 1)
    head_a = lane < HEAD_DIM
    row = lax.broadcasted_iota(jnp.int32, (blk, 2 * blk), 0)
    col = lax.broadcasted_iota(jnp.int32, (blk, 2 * blk), 1)
    band = (col >= row) & (col <= row + blk)
    bias_band = jnp.where(band, 0.0, NEG)
    bias_first = jnp.where(band & (col >= blk), 0.0, NEG)

    ks[0:blk, :] = jnp.zeros((blk, LANES), BF16)
    vs[0:blk, :] = jnp.zeros((blk, LANES), BF16)

    cp = 256
    for (_, dil) in DILATED_BRANCHES:
        nsub = seq // dil
        nb = nsub // blk
        for r in range(dil):
            for c0 in range(0, nsub, cp):
                n = min(cp, nsub - c0)
                if dil == 1:
                    src = pl.ds(c0, n)
                else:
                    src = pl.ds(r + c0 * dil, n, stride=dil)
                dst = r * nsub + c0
                qs[dst:dst + n, :] = q_ref[src, :].astype(BF16)
                ks[blk + dst:blk + dst + n, :] = k_ref[src, :].astype(BF16)
                vs[blk + dst:blk + dst + n, :] = v_ref[src, :].astype(BF16)

        if dil == 1:
            dst_acc, dst_m, dst_den = nacc, nm, nden
        else:
            dst_acc, dst_m, dst_den = racc, rm, rden

        def one_block(j, nb=nb, dst_acc=dst_acc, dst_m=dst_m, dst_den=dst_den):
            off = pl.multiple_of(j * blk, blk)
            qb = qs[pl.ds(off, blk), :]
            zero = jnp.zeros_like(qb)
            q2 = jnp.concatenate([jnp.where(head_a, qb, zero), jnp.where(head_a, zero, qb)], axis=0)
            kk = ks[pl.ds(off, 2 * blk), :]
            vv = vs[pl.ds(off, 2 * blk), :]
            s = lax.dot_general(q2, kk, (((1,), (1,)), ((), ())), preferred_element_type=F32)
            yield
            bias = jnp.where(j % nb == 0, bias_first, bias_band)
            s = s + jnp.concatenate([bias, bias], axis=0)
            mx = jnp.max(s, axis=1, keepdims=True)
            yield
            p = jnp.exp(s - mx)
            den = jnp.sum(p, axis=1, keepdims=True)
            o = jnp.dot(p.astype(BF16), vv, preferred_element_type=F32)
            yield
            dst_acc[pl.ds(off, blk), :] = jnp.where(head_a, o[:blk], o[blk:])
            dst_m[pl.ds(off, blk), :] = jnp.where(head_a, mx[:blk], mx[blk:])
            dst_den[pl.ds(off, blk), :] = jnp.where(head_a, den[:blk], den[blk:])

        def body(g, carry, one_block=one_block):
            for _ in _round_robin([one_block(g * ATTN_INTERLEAVE + u) for u in range(ATTN_INTERLEAVE)]):
                pass
            return carry

        lax.fori_loop(0, n_blocks // ATTN_INTERLEAVE, body, 0, unroll=8)

        if dil > 1:
            for r in range(dil):
                for c0 in range(0, nsub, cp):
                    n = min(cp, nsub - c0)
                    nat = pl.ds(r + c0 * dil, n, stride=dil)
                    res = pl.ds(r * nsub + c0, n)
                    m_old, m_new = nm[nat, :], rm[res, :]
                    m_all = jnp.maximum(m_old, m_new)
                    e_old = jnp.exp(m_old - m_all)
                    e_new = jnp.exp(m_new - m_all)
                    nacc[nat, :] = e_old * nacc[nat, :] + e_new * racc[res, :]
                    nden[nat, :] = e_old * nden[nat, :] + e_new * rden[res, :]
                    nm[nat, :] = m_all

    for c0 in range(0, seq, cp):
        o_ref[c0:c0 + cp, :] = nacc[c0:c0 + cp, :] / nden[c0:c0 + cp, :]


def _attn_prompt(q, k, v, buf, batch, seq):
    n_pairs = D_ATTN // LANES
    spec = pl.BlockSpec((seq, LANES), lambda b, hp: (b, hp))
    return pl.pallas_call(
        functools.partial(_attn_prompt_kernel, seq=seq),
        grid=(batch, n_pairs),
        in_specs=[spec, spec, spec, pl.BlockSpec(memory_space=pl.ANY)],
        out_specs=spec,
        out_shape=jax.ShapeDtypeStruct(buf.shape, F32),
        input_output_aliases={3: 0},
        scratch_shapes=[pltpu.VMEM((seq, LANES), BF16),
                        pltpu.VMEM((seq + ATTN_BLOCK, LANES), BF16),
                        pltpu.VMEM((seq + ATTN_BLOCK, LANES), BF16)]
                       + [pltpu.VMEM((seq, LANES), F32) for _ in range(6)],
        compiler_params=_params(("parallel", "parallel")),
        name="attn_prompt",
    )(q, k, v, buf)


def _attn_sample_kernel(q_ref, kn_ref, vn_ref, kc_ref, vc_ref, cw_ref, buf_ref, o_ref, kall, vall,
                        *, p_len, dec):
    del buf_ref
    pad_rows = kall.shape[0] - p_len
    kall[0:p_len, :] = kc_ref[...].astype(BF16)
    vall[0:p_len, :] = vc_ref[...].astype(BF16)
    tail = jnp.zeros((pad_rows - dec, D_ATTN), F32)
    kall[p_len:, :] = jnp.concatenate([kn_ref[...], tail], axis=0).astype(BF16)
    vall[p_len:, :] = jnp.concatenate([vn_ref[...], tail], axis=0).astype(BF16)

    lane = lax.broadcasted_iota(jnp.int32, (1, LANES), 1)
    head_a = lane < HEAD_DIM
    cw = cw_ref[...]
    cw2 = jnp.concatenate([cw, cw], axis=0)
    outs = []
    for hp in range(D_ATTN // LANES):
        sl = slice(hp * LANES, (hp + 1) * LANES)
        qb = q_ref[:, sl].astype(BF16)
        zero = jnp.zeros_like(qb)
        q2 = jnp.concatenate([jnp.where(head_a, qb, zero), jnp.where(head_a, zero, qb)], axis=0)
        s = lax.dot_general(q2, kall[:, sl], (((1,), (1,)), ((), ())), preferred_element_type=F32)
        s = jnp.where(cw2 > 0.0, s, NEG)
        mx = jnp.max(s, axis=1, keepdims=True)
        p = jnp.exp(s - mx) * cw2
        den = jnp.sum(p, axis=1, keepdims=True)
        o = jnp.dot(p.astype(BF16), vall[:, sl], preferred_element_type=F32) / den
        outs.append(jnp.where(head_a, o[:dec], o[dec:]))
    o_ref[...] = jnp.concatenate(outs, axis=1)


def _sample_multiplicity(key_pos, p_len, dec, n_rows):
    pos = np.full((n_rows,), -1, np.int64)
    pos[:key_pos.shape[0]] = key_pos
    t = np.arange(dec)[:, None]
    dist = p_len + t - pos[None, :]
    cw = np.zeros((dec, n_rows), np.float32)
    for (w, d) in DILATED_BRANCHES:
        cw += ((pos[None, :] >= 0) & (dist >= 0) & (dist <= w) & (dist % d == 0))
    return jnp.asarray(cw)


def _attn_sample(q, k, v, cache_k, cache_v, buf, layer, nb, dec, row0):
    p_len = cache_k.shape[2]
    n_rows = p_len + LANES
    cw = _sample_multiplicity(np.arange(p_len + dec), p_len, dec, n_rows)
    new = pl.BlockSpec((dec, D_ATTN), lambda b: (row0 // dec + b, 0))
    cache = pl.BlockSpec((None, None, p_len, D_ATTN), lambda b: (layer, b, 0, 0))
    return pl.pallas_call(
        functools.partial(_attn_sample_kernel, p_len=p_len, dec=dec),
        grid=(nb,),
        in_specs=[new, new, new, cache, cache, pl.BlockSpec((dec, n_rows), lambda b: (0, 0)),
                  pl.BlockSpec(memory_space=pl.ANY)],
        out_specs=new,
        out_shape=jax.ShapeDtypeStruct(buf.shape, F32),
        input_output_aliases={6: 0},
        scratch_shapes=[pltpu.VMEM((n_rows, D_ATTN), BF16), pltpu.VMEM((n_rows, D_ATTN), BF16)],
        compiler_params=_params(("parallel",)),
        name="attn_sample",
    )(q, k, v, cache_k, cache_v, cw, buf)


def _mlstm_kernel(*refs, rows, group):
    zm_refs, gt_refs = refs[:group], refs[group:2 * group]
    c0_ref, nm0_ref, on_ref, h_all, c1_ref, nm1_ref = refs[2 * group:2 * group + 6]
    c_rows, nm_rows = refs[2 * group + 6:3 * group + 6], refs[3 * group + 6:]
    ch = MLSTM_CHUNK
    step = pl.program_id(1)

    @pl.when(step == 0)
    def _():
        for r in range(group):
            c_rows[r][...] = c0_ref[r]
            nm_rows[r][...] = nm0_ref[r]

    def padded(a):
        if rows == ch:
            return a
        return jnp.concatenate([a, jnp.zeros((ch - rows, a.shape[1]), a.dtype)], axis=0)

    lane = lax.broadcasted_iota(jnp.int32, (1, LANES), 1)
    head_a = lane < HEAD_DIM
    ri = lax.broadcasted_iota(jnp.int32, (ch, ch), 0)
    ci = lax.broadcasted_iota(jnp.int32, (ch, ch), 1)
    causal = ci <= ri
    eye = ci == ri
    rowv = lax.broadcasted_iota(jnp.int32, (ch, 1), 0)
    valid = rowv < rows
    glane = lax.broadcasted_iota(jnp.int32, (1, LANES), 1)
    is_ig = glane < N_HEADS_MLSTM
    brow128 = lax.broadcasted_iota(jnp.int32, (LANES, 1), 0) < HEAD_DIM
    bd_mask = brow128 == head_a
    bd_pair = jnp.where(bd_mask, 1.0 / HEAD_DIM, 0.0).astype(BF16)
    ones_b = jnp.ones((ch, LANES), BF16)
    lower = causal.astype(F32)

    def one_head(hx, sub, gt, fcum, nm, qp, kb, vb, cb, n_row, res):
        mask = head_a if sub == 0 else jnp.logical_not(head_a)
        a = fcum[:, N_HEADS_MLSTM + hx:N_HEADS_MLSTM + hx + 1]
        igc = gt[:, hx:hx + 1]
        m0 = nm[2:3, hx:hx + 1]
        brow = jnp.sum(jnp.where(eye, igc - a, 0.0), axis=0, keepdims=True)
        bm = jnp.where(causal, brow, NEG)
        inter = a + m0
        m = jnp.maximum(inter, a + jnp.max(bm, axis=1, keepdims=True))
        yield
        a_rep = jnp.broadcast_to(a, (ch, LANES))
        m_rep = jnp.broadcast_to(m, (ch, LANES))
        ig_rep = jnp.broadcast_to(igc, (ch, LANES))
        am_rep = a_rep - m_rep
        w = jnp.exp(bm + am_rep)
        g_rep = jnp.exp(am_rep + m0)
        yield
        qx = jnp.where(mask, qp, 0.0)
        qxb = qx.astype(BF16)
        sc = lax.dot_general(qxb, kb, (((1,), (1,)), ((), ())), preferred_element_type=F32) * w
        yield
        num = (jnp.dot(sc.astype(BF16), vb, preferred_element_type=F32)
               + g_rep * jnp.dot(qxb, cb, preferred_element_type=F32))
        den = _split_dot(sc + g_rep * (qx * n_row), ones_b)
        yield
        m_last = m[rows - 1:rows, :]
        f_last = a[rows - 1:rows, :]
        res[sub] = (num / jnp.maximum(jnp.abs(den), jnp.exp(-m_rep)),
                    jnp.exp((f_last - m_last) + ig_rep - a_rep),
                    jnp.exp(f_last + m0 - m_last),
                    m_last)

    def one_pair(pair, zm_ref, gt, fcum, nm, c_s, nm_s, h_ref, m_last_of):
        ls = slice(pair * LANES, (pair + 1) * LANES)
        qp = padded(zm_ref[:, ls])
        kp = padded(zm_ref[:, 2 * LANES + pair * LANES:2 * LANES + (pair + 1) * LANES])
        vp = padded(zm_ref[:, 4 * LANES + pair * LANES:4 * LANES + (pair + 1) * LANES])
        op = padded(zm_ref[:, 6 * LANES + pair * LANES:6 * LANES + (pair + 1) * LANES])
        kb, vb = kp.astype(BF16), vp.astype(BF16)
        c_pair = c_s[pair]
        cb = c_pair.astype(BF16)
        n_row = nm[pair:pair + 1, :]
        res = {}
        yield from _round_robin([one_head(2 * pair + sub, sub, gt, fcum, nm, qp, kb, vb, cb, n_row, res)
                                 for sub in range(2)])
        (h_a, wl_a, decay_a, ml_a), (h_b, wl_b, decay_b, ml_b) = res[0], res[1]
        m_last_of[2 * pair], m_last_of[2 * pair + 1] = ml_a, ml_b

        kw = kp * jnp.where(head_a, wl_a, wl_b)
        upd = lax.dot_general(kw.astype(BF16), vb, (((0,), (0,)), ((), ())), preferred_element_type=F32)
        c_s[pair] = jnp.where(bd_mask, jnp.where(brow128, decay_a, decay_b) * c_pair + upd, 0.0)
        n_new = jnp.where(head_a, decay_a, decay_b) * n_row + jnp.sum(kw, axis=0, keepdims=True)
        nm_s[pair:pair + 1, :] = n_new
        yield

        y = jax.nn.sigmoid(op) * jnp.where(head_a, h_a, h_b)
        ms = _split_dot(y * y, bd_pair)
        yield
        out = y * lax.rsqrt(ms + EPS) * on_ref[:, ls]
        h_ref[:, ls] = out[:rows]

    def one_row(zm_ref, gt_ref, c_s, nm_s, h_ref):
        gt = padded(gt_ref[...])
        gt = jnp.where(valid, gt, jnp.where(is_ig, NEG, 0.0))
        lf_only = jnp.where(is_ig, 0.0, gt)
        fcum = jnp.dot(lower, lf_only, preferred_element_type=F32, precision=lax.Precision.HIGHEST)
        nm = nm_s[...]
        yield
        m_last_of = {}
        yield from _round_robin([one_pair(pair, zm_ref, gt, fcum, nm, c_s, nm_s, h_ref, m_last_of)
                                 for pair in range(N_HEADS_MLSTM // 2)])
        m_row_new = nm[2:3, :]
        for hx in range(N_HEADS_MLSTM):
            m_row_new = jnp.where(glane == hx, m_last_of[hx], m_row_new)
        nm_s[2:3, :] = m_row_new

    for _ in _round_robin([one_row(zm_refs[r], gt_refs[r], c_rows[r], nm_rows[r], h_all.at[r])
                           for r in range(group)]):
        pass

    @pl.when(step == pl.num_programs(1) - 1)
    def _():
        for r in range(group):
            c1_ref[r] = c_rows[r][...]
            nm1_ref[r] = nm_rows[r][...]


def _mlstm(zm, gt, c0, nm0, on_row, nb, seq, row0):
    rows = min(seq, MLSTM_CHUNK)
    n_chunks = seq // rows
    group = _tile(nb, MLSTM_GROUP, 1)
    toks = [(lambda g, c, r=r: (row0 // rows + (g * group + r) * n_chunks + c, 0)) for r in range(group)]
    cspec = pl.BlockSpec((group, 2, LANES, LANES), lambda g, c: (g, 0, 0, 0))
    nspec = pl.BlockSpec((group, 8, LANES), lambda g, c: (g, 0, 0))
    h, c1, nm1 = pl.pallas_call(
        functools.partial(_mlstm_kernel, rows=rows, group=group),
        grid=(nb // group, n_chunks),
        in_specs=[pl.BlockSpec((rows, 4 * D_MLSTM), t) for t in toks]
                 + [pl.BlockSpec((rows, GATE_PAD), t) for t in toks]
                 + [cspec, nspec, pl.BlockSpec((1, D_MLSTM), lambda g, c: (0, 0))],
        out_specs=[pl.BlockSpec((group, rows, D_MLSTM), lambda g, c: (g, c, 0)), cspec, nspec],
        out_shape=[jax.ShapeDtypeStruct((nb, seq, D_MLSTM), F32),
                   jax.ShapeDtypeStruct((nb, 2, LANES, LANES), F32),
                   jax.ShapeDtypeStruct((nb, 8, LANES), F32)],
        scratch_shapes=[pltpu.VMEM((2, LANES, LANES), F32) for _ in range(group)]
                       + [pltpu.VMEM((8, LANES), F32) for _ in range(group)],
        compiler_params=_params(("parallel", "arbitrary")),
        name="mlstm",
    )(*([zm] * group), *([gt] * group), c0, nm0, on_row)
    return h.reshape(nb * seq, D_MLSTM), c1, nm1


def _conv_kernel(u_ref, hist_ref, w_ref, b_ref, g_ref, nb_ref, bd_ref, buf_ref, c_ref, st_ref, *ucs,
                 seq, rt):
    del buf_ref
    lead = HIST_PAD - (CONV_WIDTH - 1)
    for k, uc in enumerate(ucs):
        ls = slice(k * LANES, (k + 1) * LANES)
        uc[0:HIST_PAD, :] = hist_ref[:, ls]
        uc[HIST_PAD:, :] = u_ref[:, ls]
    w = w_ref[...]

    def chunk(c, carry):
        base = pl.multiple_of(c * rt, rt)
        sub = min(rt, CONV_SUB_ROWS)
        done = {}

        def taps(k, s0):
            acc = None
            for j in range(CONV_WIDTH):
                term = ucs[k][pl.ds(base + (s0 + lead + j), sub), :] * w[j:j + 1, k * LANES:(k + 1) * LANES]
                acc = term if acc is None else acc + term
                if j % 4 == 3:
                    yield
            done[k, s0] = acc

        for k in range(len(ucs)):
            for _ in _round_robin([taps(k, s0) for s0 in range(0, rt, sub)]):
                pass
        y = jnp.concatenate([jnp.concatenate([done[k, s0] for s0 in range(0, rt, sub)], axis=0)
                             for k in range(len(ucs))], axis=1) + b_ref[...]
        mu = _group_mean(y, bd_ref)
        yc = y - mu
        var = _group_mean(yc * yc, bd_ref)
        z = yc * lax.rsqrt(var + EPS) * g_ref[...] + nb_ref[...]
        c_ref[pl.ds(base, rt), :] = z * jax.nn.sigmoid(z)
        return carry

    lax.fori_loop(0, seq // rt, chunk, 0, unroll=2)
    st_ref[...] = jnp.concatenate([uc[seq:seq + HIST_PAD, :] for uc in ucs], axis=1)


def _conv(u, hist, w, b, g, nbias, bd, buf, nb, seq, row0):
    rt = min(seq, 256)
    tok = lambda i: (row0 // seq + i, 0)
    const = lambda i: (0, 0)
    hspec = pl.BlockSpec((None, HIST_PAD, D_CONV), lambda i: (i, 0, 0))
    return pl.pallas_call(
        functools.partial(_conv_kernel, seq=seq, rt=rt),
        grid=(nb,),
        in_specs=[pl.BlockSpec((seq, D_CONV), tok), hspec,
                  pl.BlockSpec((HIST_PAD, D_CONV), const),
                  pl.BlockSpec((1, D_CONV), const), pl.BlockSpec((1, D_CONV), const),
                  pl.BlockSpec((1, D_CONV), const), pl.BlockSpec((D_CONV, D_CONV), const),
                  pl.BlockSpec(memory_space=pl.ANY)],
        out_specs=[pl.BlockSpec((seq, D_CONV), tok), hspec],
        out_shape=[jax.ShapeDtypeStruct(buf.shape, F32),
                   jax.ShapeDtypeStruct((nb, HIST_PAD, D_CONV), F32)],
        input_output_aliases={7: 0},
        scratch_shapes=[pltpu.VMEM((seq + HIST_PAD, LANES), F32) for _ in range(D_CONV // LANES)],
        compiler_params=_params(("parallel",)),
        name="conv",
    )(u, hist, w, b, g, nbias, bd, buf)


def _outproj_kernel(*refs, split, x_split):
    n_sub = 2 * split[0]
    n_x = n_sub if x_split else 1
    x = _group_rows(refs[:n_x], *split) if x_split else refs[0][...]
    a_ref = refs[n_x]
    m = _group_rows(refs[n_x + 1:n_x + 1 + n_sub], *split)
    c_ref, w_ref, o_ref = refs[n_x + 1 + n_sub:]
    acc = jnp.dot(a_ref[...].astype(BF16), w_ref[0:D_ATTN, :], preferred_element_type=F32)
    acc += jnp.dot(m.astype(BF16), w_ref[D_ATTN:D_ATTN + D_MLSTM, :], preferred_element_type=F32)
    acc += jnp.dot(c_ref[...].astype(BF16), w_ref[D_ATTN + D_MLSTM:, :], preferred_element_type=F32)
    o_ref[...] = x + acc


def _outproj(x, att, mo_p, mo_s, cv, w_out):
    T = att.shape[0]
    tp, ts = mo_p.shape[0], mo_s.shape[0]
    tb = _tile(np.gcd(tp, ts), 512, SUBLANES)
    per_tile = _tile(T // tb, 5, 1)
    tm = tb * per_tile
    split = (per_tile, tp // tb)
    row = lambda i: (i, 0)
    x_split = isinstance(x, tuple)
    x_specs = _group_specs(D_MODEL, tb, *split) if x_split else [pl.BlockSpec((tm, D_MODEL), row)]
    x_args = [x[0]] * per_tile + [x[1]] * per_tile if x_split else [x]
    return pl.pallas_call(
        functools.partial(_outproj_kernel, split=split, x_split=x_split),
        grid=(T // tm,),
        in_specs=x_specs + [pl.BlockSpec((tm, D_ATTN), row)] + _group_specs(D_MLSTM, tb, *split)
                 + [pl.BlockSpec((tm, D_CONV), row), pl.BlockSpec((D_MODEL, D_MODEL), lambda i: (0, 0))],
        out_specs=pl.BlockSpec((tm, D_MODEL), row),
        out_shape=jax.ShapeDtypeStruct((T, D_MODEL), F32),
        compiler_params=_params(("parallel",)),
        name="outproj",
    )(*x_args, att, *([mo_p] * per_tile), *([mo_s] * per_tile), cv, w_out)


def _ffn_kernel(x_ref, g_ref, w1_ref, w3_ref, w2_ref, o_ref, h_s, acc_s):
    j = pl.program_id(1)

    @pl.when(j == 0)
    def _():
        x = x_ref[...]
        h_s[...] = (x * lax.rsqrt(jnp.mean(x * x, axis=-1, keepdims=True) + EPS) * g_ref[...]).astype(BF16)
        acc_s[...] = x

    h = h_s[...]
    a = jnp.dot(h, w1_ref[...], preferred_element_type=F32)
    g = jnp.dot(h, w3_ref[...], preferred_element_type=F32)
    act = (a * jax.nn.sigmoid(a) * g).astype(BF16)
    acc_s[...] += jnp.dot(act, w2_ref[...], preferred_element_type=F32)

    @pl.when(j == pl.num_programs(1) - 1)
    def _():
        o_ref[...] = acc_s[...]


def _ffn(x, g, w1, w3, w2):
    T = x.shape[0]
    tm = _tile(T, 640, LANES)
    tf = D_FF // 2
    row = lambda i, j: (i, 0)
    return pl.pallas_call(
        _ffn_kernel,
        grid=(T // tm, D_FF // tf),
        in_specs=[pl.BlockSpec((tm, D_MODEL), row), pl.BlockSpec((1, D_MODEL), lambda i, j: (0, 0)),
                  pl.BlockSpec((D_MODEL, tf), lambda i, j: (0, j)),
                  pl.BlockSpec((D_MODEL, tf), lambda i, j: (0, j)),
                  pl.BlockSpec((tf, D_MODEL), lambda i, j: (j, 0))],
        out_specs=pl.BlockSpec((tm, D_MODEL), row),
        out_shape=jax.ShapeDtypeStruct((T, D_MODEL), F32),
        scratch_shapes=[pltpu.VMEM((tm, D_MODEL), BF16), pltpu.VMEM((tm, D_MODEL), F32)],
        compiler_params=_params(("parallel", "arbitrary")),
        name="ffn",
    )(x, g, w1, w3, w2)


def _to_row_tiles(ref, val):
    n = val.shape[0]
    for c in range(ROW_CHUNKS):
        ref[pl.ds(c, n, stride=ROW_CHUNKS), :] = val[:, c * LANES:(c + 1) * LANES]


def _from_row_tiles(ref, n):
    return jnp.concatenate([ref[pl.ds(c, n, stride=ROW_CHUNKS), :] for c in range(ROW_CHUNKS)], axis=1)


def _router_kernel(x_ref, g_ref, r_ref, comb_ref, xt_ref):
    x = x_ref[...]
    _to_row_tiles(xt_ref, x)
    h = x * lax.rsqrt(jnp.mean(x * x, axis=-1, keepdims=True) + EPS) * g_ref[...]
    r = r_ref[...]
    r_hi = r.astype(BF16)
    r_lo = (r - r_hi.astype(F32)).astype(BF16)
    h_hi = h.astype(BF16)
    h_lo = (h - h_hi.astype(F32)).astype(BF16)
    logits = (jnp.dot(h_hi, r_hi, preferred_element_type=F32) + jnp.dot(h_hi, r_lo, preferred_element_type=F32)
              + jnp.dot(h_lo, r_hi, preferred_element_type=F32))
    lane = lax.broadcasted_iota(jnp.int32, logits.shape, 1)
    logits = jnp.where(lane < N_EXPERTS, logits, NEG)
    v1 = jnp.max(logits, axis=1, keepdims=True)
    i1 = jnp.min(jnp.where(logits == v1, lane, LANES), axis=1, keepdims=True)
    rest = jnp.where(lane == i1, NEG, logits)
    v2 = jnp.max(rest, axis=1, keepdims=True)
    i2 = jnp.min(jnp.where(rest == v2, lane, LANES), axis=1, keepdims=True)
    e2 = jnp.exp(v2 - v1)
    inv = 1.0 / (1.0 + e2)
    route = jnp.where(lane == ROUTE_I1, i1.astype(F32), jnp.where(lane == ROUTE_I2, i2.astype(F32), 0.0))
    comb_ref[...] = jnp.where(lane == ROUTE_G1, inv, jnp.where(lane == ROUTE_G2, e2 * inv, route))


def _router(x, g, r_pad):
    T = x.shape[0]
    tm = _tile(T, 640, LANES)
    row = lambda i: (i, 0)
    return pl.pallas_call(
        _router_kernel,
        grid=(T // tm,),
        in_specs=[pl.BlockSpec((tm, D_MODEL), row), pl.BlockSpec((1, D_MODEL), lambda i: (0, 0)),
                  pl.BlockSpec((D_MODEL, LANES), lambda i: (0, 0))],
        out_specs=[pl.BlockSpec((tm, LANES), row), pl.BlockSpec((tm * ROW_CHUNKS, LANES), row)],
        out_shape=[jax.ShapeDtypeStruct((T, LANES), F32), jax.ShapeDtypeStruct((T * ROW_CHUNKS, LANES), F32)],
        compiler_params=_params(("parallel",)),
        name="router",
    )(x, g, r_pad)


def _route_plan(i1, i2, tg, n_tiles):
    e = jnp.stack([i1, i2], axis=1).reshape(-1)
    oh = (e[:, None] == jnp.arange(N_EXPERTS, dtype=jnp.int32)[None, :]).astype(jnp.int32)
    csum = jnp.cumsum(oh, axis=0)
    counts = csum[-1]
    tiles_per = (counts + tg - 1) // tg
    tile_end = jnp.cumsum(tiles_per)
    tile_start = tile_end - tiles_per
    pos = (jnp.sum(oh * (csum + (tile_start * tg)[None, :]), axis=1) - 1).astype(jnp.int32)
    tile_ids = jnp.arange(n_tiles, dtype=jnp.int32)
    tile_expert = jnp.minimum(jnp.sum((tile_ids[:, None] >= tile_end[None, :]).astype(jnp.int32), axis=1),
                              N_EXPERTS - 1).astype(jnp.int32)
    tile_valid = (tile_ids < tile_end[-1]).astype(jnp.int32)
    last_tile = jnp.where(tiles_per > 0, tile_end - 1, -1)
    spare = n_tiles - N_EXPERTS + jnp.arange(N_EXPERTS, dtype=jnp.int32)
    spare = jnp.where(spare >= tile_end[-1], spare, -1)
    fill_tiles = jnp.concatenate([last_tile, spare]).astype(jnp.int32)
    return pos, tile_expert, tile_valid, fill_tiles


def _moe_scatter_kernel(pos_ref, fill_ref, xt_ref, xs_hbm, zeros, sem, zsem, *, tm, tg):
    base = pl.program_id(0) * tm

    @pl.when(pl.program_id(0) == 0)
    def _():
        zeros[...] = jnp.zeros(zeros.shape, F32)

        def fill(e):
            row = pl.multiple_of(fill_ref[e] * (tg * ROW_CHUNKS), tg * ROW_CHUNKS)
            return pltpu.make_async_copy(zeros, xs_hbm.at[pl.ds(row, tg * ROW_CHUNKS), :], zsem)

        for e in range(2 * N_EXPERTS):
            @pl.when(fill_ref[e] >= 0)
            def _(e=e):
                fill(e).start()

        for e in range(2 * N_EXPERTS):
            @pl.when(fill_ref[e] >= 0)
            def _(e=e):
                fill(e).wait()

    def issue(r, carry):
        src = xt_ref.at[pl.ds(pl.multiple_of(r * ROW_CHUNKS, ROW_CHUNKS), ROW_CHUNKS), :]
        for choice in range(2):
            p = pos_ref[2 * (base + r) + choice]
            dst = xs_hbm.at[pl.ds(pl.multiple_of(p * ROW_CHUNKS, ROW_CHUNKS), ROW_CHUNKS), :]
            pltpu.make_async_copy(src, dst, sem.at[choice]).start(priority=choice)
        return carry

    lax.fori_loop(0, tm, issue, 0, unroll=8)
    for choice in range(2):
        pltpu.make_async_copy(xt_ref, xs_hbm.at[pl.ds(0, tm * ROW_CHUNKS), :], sem.at[choice]).wait()


def _moe_scatter(xt, pos, last_tile, n_rows, tg):
    T = xt.shape[0] // ROW_CHUNKS
    tm = _tile(T, 640, LANES)
    return pl.pallas_call(
        functools.partial(_moe_scatter_kernel, tm=tm, tg=tg),
        grid_spec=pltpu.PrefetchScalarGridSpec(
            num_scalar_prefetch=2,
            grid=(T // tm,),
            in_specs=[pl.BlockSpec((tm * ROW_CHUNKS, LANES), lambda i, pos, last: (i, 0))],
            out_specs=pl.BlockSpec(memory_space=pl.ANY),
            scratch_shapes=[pltpu.VMEM((tg * ROW_CHUNKS, LANES), F32),
                            pltpu.SemaphoreType.DMA((2,)), pltpu.SemaphoreType.DMA(())]),
        out_shape=jax.ShapeDtypeStruct((n_rows * ROW_CHUNKS, LANES), F32),
        compiler_params=_params(("arbitrary",)),
        name="moe_scatter",
    )(pos, last_tile, xt)


def _moe_group_kernel(te_ref, tv_ref, xs_ref, g_ref, w1_ref, w3_ref, w2_ref, o_ref, *, tg):
    del te_ref
    i = pl.program_id(0)

    @pl.when(tv_ref[i] > 0)
    def _():
        x = _from_row_tiles(xs_ref, tg)
        h = (x * lax.rsqrt(jnp.mean(x * x, axis=-1, keepdims=True) + EPS) * g_ref[...]).astype(BF16)
        acc = jnp.zeros((tg, D_MODEL), F32)
        for c in range(0, D_FF, MOE_FF_CHUNK):
            a = jnp.dot(h, w1_ref[:, c:c + MOE_FF_CHUNK], preferred_element_type=F32)
            g = jnp.dot(h, w3_ref[:, c:c + MOE_FF_CHUNK], preferred_element_type=F32)
            act = (a * jax.nn.sigmoid(a) * g).astype(BF16)
            acc = acc + jnp.dot(act, w2_ref[c:c + MOE_FF_CHUNK, :], preferred_element_type=F32)
        _to_row_tiles(o_ref, acc)

    @pl.when(tv_ref[i] == 0)
    def _():
        o_ref[...] = jnp.zeros((tg * ROW_CHUNKS, LANES), F32)


def _moe_group(xs, g, w1, w3, w2, tile_expert, tile_valid, tg, n_tiles):
    wspec = lambda shape: pl.BlockSpec((None,) + shape, lambda i, te, tv: (te[i], 0, 0))
    rows = pl.BlockSpec((tg * ROW_CHUNKS, LANES), lambda i, te, tv: (i, 0))
    rows_in = pl.BlockSpec((tg * ROW_CHUNKS, LANES), lambda i, te, tv: (i * tv[i], 0))
    return pl.pallas_call(
        functools.partial(_moe_group_kernel, tg=tg),
        grid_spec=pltpu.PrefetchScalarGridSpec(
            num_scalar_prefetch=2,
            grid=(n_tiles,),
            in_specs=[rows_in, pl.BlockSpec((1, D_MODEL), lambda i, te, tv: (0, 0)),
                      wspec((D_MODEL, D_FF)), wspec((D_MODEL, D_FF)), wspec((D_FF, D_MODEL))],
            out_specs=rows),
        out_shape=jax.ShapeDtypeStruct((n_tiles * tg * ROW_CHUNKS, LANES), F32),
        compiler_params=_params(("arbitrary",)),
        name="moe_group",
    )(tile_expert, tile_valid, xs, g, w1, w3, w2)


def _moe_combine_kernel(pos_ref, x_ref, route_ref, ys_hbm, op_ref, os_ref, yb, sem, *, tm, n_steps, n_prompt):
    i = pl.program_id(0)

    def gather(step, slot):
        base = step * (2 * tm)

        def issue(r, carry):
            for choice in range(2):
                p = pos_ref[base + 2 * r + choice]
                src = ys_hbm.at[pl.ds(pl.multiple_of(p * ROW_CHUNKS, ROW_CHUNKS), ROW_CHUNKS), :]
                dst = yb.at[slot, choice, pl.ds(pl.multiple_of(r * ROW_CHUNKS, ROW_CHUNKS), ROW_CHUNKS), :]
                pltpu.make_async_copy(src, dst, sem.at[slot]).start(priority=choice)
            return carry

        lax.fori_loop(0, tm, issue, 0, unroll=8)

    @pl.when(i == 0)
    def _():
        gather(0, 0)

    @pl.when(i + 1 < n_steps)
    def _():
        gather(i + 1, (i + 1) % 2)

    slot = i % 2
    pltpu.make_async_copy(yb.at[slot], yb.at[slot], sem.at[slot]).wait()
    route = route_ref[...]
    g1 = route[:, ROUTE_G1:ROUTE_G1 + 1]
    g2 = route[:, ROUTE_G2:ROUTE_G2 + 1]
    res = x_ref[...] + g1 * _from_row_tiles(yb.at[slot, 0], tm) + g2 * _from_row_tiles(yb.at[slot, 1], tm)

    @pl.when(i < n_prompt)
    def _():
        op_ref[...] = res

    @pl.when(i >= n_prompt)
    def _():
        os_ref[...] = res


def _moe_combine(x, route, ys, pos, tp):
    T = x.shape[0]
    tm = _tile(np.gcd(tp, T - tp), 512, SUBLANES)
    n_steps = T // tm
    n_prompt = tp // tm
    row = lambda i, pos: (i, 0)
    return pl.pallas_call(
        functools.partial(_moe_combine_kernel, tm=tm, n_steps=n_steps, n_prompt=n_prompt),
        grid_spec=pltpu.PrefetchScalarGridSpec(
            num_scalar_prefetch=1,
            grid=(n_steps,),
            in_specs=[pl.BlockSpec((tm, D_MODEL), row), pl.BlockSpec((tm, LANES), row),
                      pl.BlockSpec(memory_space=pl.ANY)],
            out_specs=[pl.BlockSpec((tm, D_MODEL), lambda i, pos: (jnp.minimum(i, n_prompt - 1), 0)),
                       pl.BlockSpec((tm, D_MODEL), lambda i, pos: (jnp.maximum(i - n_prompt, 0), 0))],
            scratch_shapes=[pltpu.VMEM((2, 2, tm * ROW_CHUNKS, LANES), F32), pltpu.SemaphoreType.DMA((2,))]),
        out_shape=[jax.ShapeDtypeStruct((tp, D_MODEL), F32), jax.ShapeDtypeStruct((T - tp, D_MODEL), F32)],
        compiler_params=_params(("arbitrary",)),
        name="moe_combine",
    )(pos, x, route, ys)


def _moe(x, g, r_pad, w1, w3, w2, tp):
    T = x.shape[0]
    route, xt = _router(x, g, r_pad)
    i1 = route[:, ROUTE_I1].astype(jnp.int32)
    i2 = route[:, ROUTE_I2].astype(jnp.int32)
    tg = MOE_TILE
    n_tiles = pl.cdiv(2 * T, tg) + N_EXPERTS
    pos, tile_expert, tile_valid, last_tile = _route_plan(i1, i2, tg, n_tiles)
    xs = _moe_scatter(xt, pos, last_tile, n_tiles * tg, tg)
    ys = _moe_group(xs, g, w1, w3, w2, tile_expert, tile_valid, tg, n_tiles)
    return _moe_combine(x, route, ys, pos, tp)


def _rope_tables(pos):
    half = ROPE_DIM // 2
    inv_freq = np.float32(ROPE_THETA) ** (np.float32(-2.0) * np.arange(half, dtype=np.float32) / np.float32(ROPE_DIM))
    ang = (pos.astype(np.float32)[:, None] * inv_freq[None, :]).astype(np.float64)
    cos, sin = np.cos(ang).astype(np.float32), np.sin(ang).astype(np.float32)
    n = pos.shape[0]
    ones = np.ones((n, HEAD_DIM - ROPE_DIM), np.float32)
    zeros = np.zeros((n, HEAD_DIM - ROPE_DIM), np.float32)
    zh = np.zeros((n, half), np.float32)
    c = np.concatenate([cos, cos, ones], axis=1)
    s1 = np.concatenate([zh, sin, zeros], axis=1)
    s2 = np.concatenate([-sin, zh, zeros], axis=1)
    two = lambda t: jnp.asarray(np.concatenate([t, t], axis=1))
    return two(c), two(s1), two(s2)


def _kv_out_kernel(*refs, depth):
    k_refs, v_refs = refs[:depth], refs[depth:2 * depth]
    ko_ref, vo_ref = refs[2 * depth:]
    layer = pl.program_id(0)
    for d in range(depth):
        @pl.when(layer == d)
        def _(d=d):
            kx, vx = k_refs[d][...], v_refs[d][...]
            n = kx.shape[0]
            for h in range(SUBLANES):
                dst = pl.ds(h, n, stride=SUBLANES)
                if h < N_HEADS_ATTN:
                    ko_ref[dst, :] = kx[:, h * HEAD_DIM:(h + 1) * HEAD_DIM]
                    vo_ref[dst, :] = vx[:, h * HEAD_DIM:(h + 1) * HEAD_DIM]
                else:
                    ko_ref[dst, :] = jnp.zeros((n, HEAD_DIM), F32)
                    vo_ref[dst, :] = jnp.zeros((n, HEAD_DIM), F32)


def _kv_out(ks, vs, batch, seq, keep):
    depth = len(ks)
    rows = _tile(keep, 512, SUBLANES)
    first = seq - keep
    srcs = [pl.BlockSpec((rows, D_ATTN),
                         lambda l, b, j, d=d: (jnp.where(l == d, (b * seq + first) // rows + j, 0), 0))
            for d in range(depth)]
    dst = pl.BlockSpec((None, None, rows * SUBLANES, HEAD_DIM), lambda l, b, j: (l, b, j, 0))
    shape = jax.ShapeDtypeStruct((depth, batch, keep * SUBLANES, HEAD_DIM), F32)
    k8, v8 = pl.pallas_call(
        functools.partial(_kv_out_kernel, depth=depth),
        grid=(depth, batch, keep // rows),
        in_specs=srcs + srcs,
        out_specs=[dst, dst],
        out_shape=[shape, shape],
        compiler_params=_params(("parallel", "parallel", "parallel")),
        name="kv_out",
    )(*ks, *vs)
    heads = lambda t: t.reshape(depth, batch, keep, SUBLANES, HEAD_DIM)[:, :, :, :N_HEADS_ATTN, :]
    return heads(k8), heads(v8)


def _pair_state(c, n, m):
    nb = c.shape[0]
    c = c.reshape(nb, 2, 2, HEAD_DIM, HEAD_DIM)
    z = jnp.zeros((nb, 2, HEAD_DIM, HEAD_DIM), F32)
    top = jnp.concatenate([c[:, :, 0], z], axis=-1)
    bot = jnp.concatenate([z, c[:, :, 1]], axis=-1)
    cp = jnp.concatenate([top, bot], axis=-2)
    nm = jnp.zeros((nb, 8, LANES), F32)
    nm = nm.at[:, 0:2, :].set(n.reshape(nb, 2, LANES))
    nm = nm.at[:, 2, 0:N_HEADS_MLSTM].set(m)
    return cp, nm


def _unpair_state(cp, nm):
    nb = cp.shape[0]
    c = jnp.stack([cp[:, :, :HEAD_DIM, :HEAD_DIM], cp[:, :, HEAD_DIM:, HEAD_DIM:]], axis=2)
    c = c.reshape(nb, N_HEADS_MLSTM, HEAD_DIM, HEAD_DIM)
    n = nm[:, 0:2, :].reshape(nb, N_HEADS_MLSTM, HEAD_DIM)
    m = nm[:, 2, 0:N_HEADS_MLSTM]
    return c, n, m


def kernel(x_prompt, x_sample, cache_attn_k, cache_attn_v, state_mlstm_C, state_mlstm_n, state_mlstm_m, state_conv, norm_mix, w_in, q_norm, k_norm, mlstm_gate_bias, mlstm_out_norm, conv_dw_w, conv_dw_b, conv_norm_g, conv_norm_b, w_out, norm_ffn, ffn_w1, ffn_w3, ffn_w2, moe_router, moe_w1, moe_w3, moe_w2):
    batch, seq, _ = x_prompt.shape
    nbs, dec, _ = x_sample.shape
    depth = w_in.shape[0]
    tp = batch * seq
    T = tp + nbs * dec
    keep = min(max(w for w, _ in DILATED_BRANCHES), seq)
    hist_rows = CONV_WIDTH - 1

    x = (x_prompt.reshape(tp, D_MODEL), x_sample.reshape(nbs * dec, D_MODEL))

    pos = np.concatenate([np.tile(np.arange(seq, dtype=np.float32), batch),
                          np.tile(np.float32(PAST_LEN) + np.arange(dec, dtype=np.float32), nbs)])
    rc, rs1, rs2 = _rope_tables(pos)
    gi = np.arange(D_ATTN) // HEAD_DIM
    bd = jnp.asarray((gi[:, None] == gi[None, :]).astype(np.float32) / HEAD_DIM, dtype=BF16)
    p_len = cache_attn_k.shape[2]
    cache_k = cache_attn_k.reshape(depth, nbs, p_len, D_ATTN)
    cache_v = cache_attn_v.reshape(depth, nbs, p_len, D_ATTN)

    zero_c = jnp.zeros((batch, 2, LANES, LANES), F32)
    zero_nm = jnp.zeros((batch, 8, LANES), F32)
    zero_hist = jnp.zeros((batch, HIST_PAD, D_CONV), F32)

    outs = [[] for _ in range(10)]
    ks, vs = [], []
    for l in range(depth):
        o = 3 * D_ATTN + 4 * D_MLSTM
        w_p = jnp.concatenate([w_in[l][:, :o], jnp.pad(w_in[l][:, o:o + 8], ((0, 0), (0, GATE_PAD - 8))),
                               w_in[l][:, o + 8:]], axis=1).astype(BF16)
        qn_row = jnp.tile(q_norm[l], N_HEADS_ATTN)[None, :]
        kn_row = jnp.tile(k_norm[l], N_HEADS_ATTN)[None, :]
        gb_row = jnp.pad(mlstm_gate_bias[l], (0, GATE_PAD - 2 * N_HEADS_MLSTM))[None, :]

        q, k, v, zm, gt, u = _inproj(x, norm_mix[l][None, :], w_p, qn_row, kn_row, gb_row, bd, rc, rs1, rs2)

        on_row = mlstm_out_norm[l][None, :]
        w_conv = jnp.pad(conv_dw_w[l], ((0, HIST_PAD - CONV_WIDTH), (0, 0)))
        conv_args = (w_conv, conv_dw_b[l][None, :], conv_norm_g[l][None, :], conv_norm_b[l][None, :], bd)

        att = _attn_sample(q, k, v, cache_k, cache_v, jnp.zeros((T, D_ATTN), F32), l, nbs, dec, tp)
        c0_s, nm0_s = _pair_state(state_mlstm_C[l], state_mlstm_n[l], state_mlstm_m[l])
        hm_s, c1_s, nm1_s = _mlstm(zm, gt, c0_s, nm0_s, on_row, nbs, dec, tp)
        hist_s = jnp.pad(state_conv[l], ((0, 0), (HIST_PAD - hist_rows, 0), (0, 0)))
        cv, st_s = _conv(u, hist_s, *conv_args, jnp.zeros((T, D_CONV), F32), nbs, dec, tp)

        att = _attn_prompt(q, k, v, att, batch, seq)
        hm_p, c1_p, nm1_p = _mlstm(zm, gt, zero_c, zero_nm, on_row, batch, seq, 0)
        cv, st_p = _conv(u, zero_hist, *conv_args, cv, batch, seq, 0)

        x = _outproj(x, att, hm_p, hm_s, cv, w_out[l].astype(BF16))

        i = l // 2
        if l % 2 == 0:
            x = _ffn(x, norm_ffn[l][None, :], ffn_w1[i].astype(BF16), ffn_w3[i].astype(BF16),
                     ffn_w2[i].astype(BF16))
        else:
            r_pad = jnp.pad(moe_router[i], ((0, 0), (0, LANES - N_EXPERTS)))
            parts = _moe(x, norm_ffn[l][None, :], r_pad, moe_w1[i].astype(BF16), moe_w3[i].astype(BF16),
                         moe_w2[i].astype(BF16), tp)
            x = None if l == depth - 1 else jnp.concatenate(parts, axis=0)

        ks.append(k)
        vs.append(v)
        cp_, np_, mp_ = _unpair_state(c1_p, nm1_p)
        cs_, ns_, ms_ = _unpair_state(c1_s, nm1_s)
        layer_outs = (k[tp:].reshape(nbs, dec, N_HEADS_ATTN, HEAD_DIM),
                      v[tp:].reshape(nbs, dec, N_HEADS_ATTN, HEAD_DIM),
                      cp_, np_, mp_, cs_, ns_, ms_,
                      st_p[:, HIST_PAD - hist_rows:], st_s[:, HIST_PAD - hist_rows:])
        for lst, val in zip(outs, layer_outs):
            lst.append(val)

    k_prompt, v_prompt = _kv_out(ks, vs, batch, seq, keep)
    if x is not None:
        parts = (x[:tp], x[tp:])
    y_prompt = parts[0].reshape(batch, seq, D_MODEL)
    y_sample = parts[1].reshape(nbs, dec, D_MODEL)
    return (y_prompt, y_sample, k_prompt, v_prompt) + tuple(jnp.stack(lst, axis=0) for lst in outs)
```
